```python
import jax, jax.numpy as jnp
from jax import lax
import numpy as np

D_MODEL = 1024
BATCH = 2
SEQ = 8192
DEPTH = 4

GRID_W = 64
CTX_LEN = 256
N_MOD = 6
EPS = 1e-6
D_CONV = 512
CONV_K = 31
ATT_HEADS = 8
ATT_KV_HEADS = 2
ATT_HD = 64
ATT_GROUPS = ATT_HEADS // ATT_KV_HEADS
ROPE_THETA = 10000.0
Q_BLOCK = 128
RET_HEADS = 8
RET_DK = 64
RET_DV = 64
RET_CHUNK = 128
D_FF = 2816
FFN_K = 3
N_BRANCH = 3
IN_SIZES = (2 * D_CONV, ATT_HEADS * ATT_HD, ATT_KV_HEADS * ATT_HD, ATT_KV_HEADS * ATT_HD,
            RET_HEADS * RET_DK, RET_HEADS * RET_DK, RET_HEADS * RET_DV, RET_HEADS * RET_DV,
            N_BRANCH * D_MODEL)
IN_COLS = sum(IN_SIZES)
IN_SPLITS = tuple(sum(IN_SIZES[:i + 1]) for i in range(len(IN_SIZES) - 1))

kernel_name = "hybrid_conv_gqa_retention_dit_block"


def rms_norm(x, g):
    xf = x.astype(jnp.float32)
    y = xf * lax.rsqrt(jnp.mean(xf * xf, axis=-1, keepdims=True) + EPS)
    return (y * g.astype(jnp.float32)).astype(x.dtype)


def layer_norm(x, g, b):
    xf = x.astype(jnp.float32)
    mu = jnp.mean(xf, axis=-1, keepdims=True)
    xc = xf - mu
    y = xc * lax.rsqrt(jnp.mean(xc * xc, axis=-1, keepdims=True) + EPS)
    return (y * g.astype(jnp.float32) + b.astype(jnp.float32)).astype(x.dtype)


def modulate(h, shift, scale):
    return h * (1.0 + scale) + shift


def heads(t, n):
    return t.reshape(t.shape[:-1] + (n, t.shape[-1] // n))


def dwconv(x, w, b):
    k = w.shape[0]
    pad = k // 2
    y = lax.conv_general_dilated(x, w[:, None, :].astype(x.dtype), (1,), [(pad, pad)],
                                 dimension_numbers=('NWC', 'WIO', 'NWC'),
                                 feature_group_count=x.shape[-1])
    return y + b.astype(x.dtype)


def axial_rope_tables(seq_len):
    rows = seq_len // GRID_W
    row = jnp.repeat(jnp.arange(rows, dtype=jnp.float32), GRID_W)
    col = jnp.tile(jnp.arange(GRID_W, dtype=jnp.float32), rows)
    n_freq = ATT_HD // 4
    inv = ROPE_THETA ** (-jnp.arange(n_freq, dtype=jnp.float32) / n_freq)
    ang_r = row[:, None] * inv[None, :]
    ang_c = col[:, None] * inv[None, :]
    return (jnp.cos(ang_r), jnp.sin(ang_r), jnp.cos(ang_c), jnp.sin(ang_c))


def _rotate(x, cos, sin):
    half = x.shape[-1] // 2
    x1, x2 = x[..., :half], x[..., half:]
    cs, sn = cos[None, :, None, :], sin[None, :, None, :]
    return jnp.concatenate([x1 * cs - x2 * sn, x2 * cs + x1 * sn], axis=-1)


def apply_axial_rope(x, tabs):
    cos_r, sin_r, cos_c, sin_c = tabs
    half = ATT_HD // 2
    xf = x.astype(jnp.float32)
    out = jnp.concatenate([_rotate(xf[..., :half], cos_r, sin_r),
                           _rotate(xf[..., half:], cos_c, sin_c)], axis=-1)
    return out.astype(x.dtype)


def gqa_attend(q, k, v):
    b, lq = q.shape[:2]
    qg = q.reshape(b, lq, ATT_KV_HEADS, ATT_GROUPS, ATT_HD)
    s = jnp.einsum('bqkgd,bskd->bkgqs', qg, k).astype(jnp.float32) * (ATT_HD ** -0.5)
    p = jax.nn.softmax(s, axis=-1).astype(v.dtype)
    o = jnp.einsum('bkgqs,bskd->bqkgd', p, v)
    return o.reshape(b, lq, ATT_HEADS * ATT_HD)


def latent_attention(q, k_lat, v_lat, k_ctx, v_ctx):
    b, l = q.shape[:2]
    k_all = jnp.concatenate([k_ctx, k_lat], axis=1)
    v_all = jnp.concatenate([v_ctx, v_lat], axis=1)
    nb = l // Q_BLOCK
    qb = q.reshape(b, nb, Q_BLOCK, ATT_HEADS, ATT_HD).swapaxes(0, 1)
    o = lax.map(lambda qi: gqa_attend(qi, k_all, v_all), qb)
    return o.swapaxes(0, 1).reshape(b, l, ATT_HEADS * ATT_HD)


def retention_scan(q, k, v, log_gamma, s0):
    dt = v.dtype
    q, k, v = q.astype(jnp.float32), k.astype(jnp.float32), v.astype(jnp.float32)
    log_gamma = log_gamma.astype(jnp.float32)
    b, l, h, _ = q.shape
    dv = v.shape[-1]
    n = l // RET_CHUNK

    def to_chunks(a):
        return a.reshape(b, n, RET_CHUNK, h, a.shape[-1]).transpose(1, 0, 2, 3, 4)

    idx = jnp.arange(RET_CHUNK, dtype=jnp.float32)
    rel = idx[:, None] - idx[None, :]
    lower = rel >= 0
    decay_in = jnp.where(lower[None], jnp.exp(jnp.where(lower, rel, 0.0)[None] * log_gamma[:, None, None]), 0.0)
    q_dec = jnp.exp((idx + 1.0)[:, None] * log_gamma[None, :])
    k_dec = jnp.exp((RET_CHUNK - 1.0 - idx)[:, None] * log_gamma[None, :])
    chunk_dec = jnp.exp(RET_CHUNK * log_gamma)

    def step(state, inp):
        qc, kc, vc = inp
        inner = jnp.einsum('bihd,bjhd->bhij', qc, kc) * decay_in
        o = (jnp.einsum('bhij,bjhe->bihe', inner, vc)
             + jnp.einsum('bihd,bhde->bihe', qc * q_dec[:, :, None], state))
        state = state * chunk_dec[:, None, None] + jnp.einsum('bjhd,bjhe->bhde', kc * k_dec[:, :, None], vc)
        return state, o

    state, o = lax.scan(step, s0.astype(jnp.float32), (to_chunks(q), to_chunks(k), to_chunks(v)))
    o = o.transpose(1, 0, 2, 3, 4).reshape(b, l, h, dv)
    return o.astype(dt), state


def bidir_retention(q, k, v, q_c, k_c, v_c, log_gamma):
    b = q.shape[0]
    s0 = jnp.zeros((b, RET_HEADS, RET_DK, RET_DV), jnp.float32)
    flip = lambda a: a[:, ::-1]
    o_cf, s_f = retention_scan(q_c, k_c, v_c, log_gamma[0], s0)
    o_cb, s_b = retention_scan(flip(q_c), flip(k_c), flip(v_c), log_gamma[1], s0)
    o_lf, _ = retention_scan(q, k, v, log_gamma[0], s_f)
    o_lb, _ = retention_scan(flip(q), flip(k), flip(v), log_gamma[1], s_b)
    return o_lf + flip(o_lb), o_cf + flip(o_cb)


def retention_output(o, gate, g):
    of = o.astype(jnp.float32)
    mu = jnp.mean(of, axis=-1, keepdims=True)
    oc = of - mu
    y = oc * lax.rsqrt(jnp.mean(oc * oc, axis=-1, keepdims=True) + EPS)
    y = y.reshape(o.shape[:2] + (RET_HEADS * RET_DV,)) * g.astype(jnp.float32)
    return y.astype(o.dtype) * jax.nn.silu(gate)


def conv_module(glu_in, dw_w, dw_b, ln_g, ln_b):
    a, bg = jnp.split(glu_in, 2, axis=-1)
    u = dwconv(a * jax.nn.sigmoid(bg), dw_w, dw_b)
    return jax.nn.silu(layer_norm(u, ln_g, ln_b))


def merge_branches(gates, a, att, ret, w_pa, w_pb, w_pc):
    ga, gb, gc = jnp.split(jax.nn.sigmoid(gates), N_BRANCH, axis=-1)
    return ga * (a @ w_pa) + gb * (att @ w_pb) + gc * (ret @ w_pc)


def conv_ffn(h, w_up, dw_w, dw_b, w_down):
    gt, val = jnp.split(h @ w_up, 2, axis=-1)
    gt = dwconv(gt, dw_w, dw_b)
    return (jax.nn.silu(gt) * val) @ w_down


def setup_inputs(seed: int = 0) -> dict:
    key = jax.random.key(seed)
    ks = jax.random.split(key, 32)
    f32 = jnp.float32
    nrm = lambda k, shape, s: jax.random.normal(k, shape, f32) * s
    base_logit = jnp.log(2.0 ** (5.0 + jnp.arange(RET_HEADS, dtype=f32)) - 1.0)
    return {
        "x": nrm(ks[0], (BATCH, SEQ, D_MODEL), 1.0),
        "c": nrm(ks[1], (BATCH, D_MODEL), 1.0),
        "ctx": nrm(ks[2], (BATCH, CTX_LEN, D_MODEL), 1.0),
        "c_ctx": nrm(ks[3], (D_MODEL,), 1.0),
        "w_ada": nrm(ks[4], (DEPTH, D_MODEL, N_MOD * D_MODEL), 0.5 * D_MODEL ** -0.5),
        "b_ada": nrm(ks[5], (DEPTH, N_MOD * D_MODEL), 0.02),
        "norm1_g": 1.0 + nrm(ks[6], (DEPTH, D_MODEL), 0.02),
        "w_in": nrm(ks[7], (DEPTH, D_MODEL, IN_COLS), D_MODEL ** -0.5),
        "b_in": nrm(ks[8], (DEPTH, IN_COLS), 0.02),
        "conv_dw_w": nrm(ks[9], (DEPTH, CONV_K, D_CONV), CONV_K ** -0.5),
        "conv_dw_b": nrm(ks[10], (DEPTH, D_CONV), 0.02),
        "conv_ln_g": 1.0 + nrm(ks[11], (DEPTH, D_CONV), 0.02),
        "conv_ln_b": nrm(ks[12], (DEPTH, D_CONV), 0.02),
        "q_norm_g": 1.0 + nrm(ks[13], (DEPTH, ATT_HD), 0.02),
        "k_norm_g": 1.0 + nrm(ks[14], (DEPTH, ATT_HD), 0.02),
        "ret_decay_logit": base_logit[None, None, :] + nrm(ks[15], (DEPTH, 2, RET_HEADS), 0.1),
        "ret_gn_g": 1.0 + nrm(ks[16], (DEPTH, RET_HEADS * RET_DV), 0.02),
        "w_pa": nrm(ks[17], (DEPTH, D_CONV, D_MODEL), D_CONV ** -0.5),
        "w_pb": nrm(ks[18], (DEPTH, ATT_HEADS * ATT_HD, D_MODEL), (ATT_HEADS * ATT_HD) ** -0.5),
        "w_pc": nrm(ks[19], (DEPTH, RET_HEADS * RET_DV, D_MODEL), (RET_HEADS * RET_DV) ** -0.5),
        "w_out": nrm(ks[20], (DEPTH, D_MODEL, D_MODEL), D_MODEL ** -0.5),
        "norm2_g": 1.0 + nrm(ks[21], (DEPTH, D_MODEL), 0.02),
        "w_up": nrm(ks[22], (DEPTH, D_MODEL, 2 * D_FF), D_MODEL ** -0.5),
        "ffn_dw_w": nrm(ks[23], (DEPTH, FFN_K, D_FF), FFN_K ** -0.5),
        "ffn_dw_b": nrm(ks[24], (DEPTH, D_FF), 0.02),
        "w_down": nrm(ks[25], (DEPTH, D_FF, D_MODEL), D_FF ** -0.5),
        "final_norm_g": 1.0 + nrm(ks[26], (D_MODEL,), 0.02),
    }


def reference(x, c, ctx, c_ctx, w_ada, b_ada, norm1_g, w_in, b_in, conv_dw_w, conv_dw_b,
              conv_ln_g, conv_ln_b, q_norm_g, k_norm_g, ret_decay_logit, ret_gn_g,
              w_pa, w_pb, w_pc, w_out, norm2_g, w_up, ffn_dw_w, ffn_dw_b, w_down, final_norm_g):
    seq_len = x.shape[1]
    rope = axial_rope_tables(seq_len)
    xc = ctx
    silu_c = jax.nn.silu(c)
    silu_cc = jax.nn.silu(c_ctx)
    for l in range(DEPTH):
        last = l == DEPTH - 1
        mod_l = (silu_c @ w_ada[l] + b_ada[l])[:, None, :]
        mod_c = silu_cc @ w_ada[l] + b_ada[l]
        sh1, sc1, g1, sh2, sc2, g2 = jnp.split(mod_l, N_MOD, axis=-1)
        csh1, csc1, cg1, csh2, csc2, cg2 = jnp.split(mod_c, N_MOD, axis=-1)

        h = modulate(rms_norm(x, norm1_g[l]), sh1, sc1)
        hc = modulate(rms_norm(xc, norm1_g[l]), csh1, csc1)
        glu, q, k, v, rq, rk, rv, rg, gates = jnp.split(h @ w_in[l] + b_in[l], IN_SPLITS, axis=-1)
        glu_c, q_c, k_c, v_c, rq_c, rk_c, rv_c, rg_c, gates_c = jnp.split(hc @ w_in[l] + b_in[l], IN_SPLITS, axis=-1)

        a = conv_module(glu, conv_dw_w[l], conv_dw_b[l], conv_ln_g[l], conv_ln_b[l])

        q = apply_axial_rope(rms_norm(heads(q, ATT_HEADS), q_norm_g[l]), rope)
        k = apply_axial_rope(rms_norm(heads(k, ATT_KV_HEADS), k_norm_g[l]), rope)
        v = heads(v, ATT_KV_HEADS)
        k_c = rms_norm(heads(k_c, ATT_KV_HEADS), k_norm_g[l])
        v_c = heads(v_c, ATT_KV_HEADS)
        att = latent_attention(q, k, v, k_c, v_c)

        log_gamma = jax.nn.log_sigmoid(ret_decay_logit[l].astype(jnp.float32))
        scale_k = RET_DK ** -0.5
        ret, ret_c = bidir_retention(heads(rq, RET_HEADS), heads(rk, RET_HEADS) * scale_k, heads(rv, RET_HEADS),
                                     heads(rq_c, RET_HEADS), heads(rk_c, RET_HEADS) * scale_k, heads(rv_c, RET_HEADS),
                                     log_gamma)
        ret = retention_output(ret, rg, ret_gn_g[l])

        y = merge_branches(gates, a, att, ret, w_pa[l], w_pb[l], w_pc[l]) @ w_out[l]
        x = x + g1 * y
        if not last:
            a_c = conv_module(glu_c, conv_dw_w[l], conv_dw_b[l], conv_ln_g[l], conv_ln_b[l])
            q_c = rms_norm(heads(q_c, ATT_HEADS), q_norm_g[l])
            att_c = gqa_attend(q_c, k_c, v_c)
            ret_c = retention_output(ret_c, rg_c, ret_gn_g[l])
            y_c = merge_branches(gates_c, a_c, att_c, ret_c, w_pa[l], w_pb[l], w_pc[l]) @ w_out[l]
            xc = xc + cg1 * y_c

        x = x + g2 * conv_ffn(modulate(rms_norm(x, norm2_g[l]), sh2, sc2),
                              w_up[l], ffn_dw_w[l], ffn_dw_b[l], w_down[l])
        if not last:
            xc = xc + cg2 * conv_ffn(modulate(rms_norm(xc, norm2_g[l]), csh2, csc2),
                                     w_up[l], ffn_dw_w[l], ffn_dw_b[l], w_down[l])
    return rms_norm(x, final_norm_g)
```

```python
import functools

import jax
import jax.numpy as jnp
from jax import lax
from jax.experimental import pallas as pl
from jax.experimental.pallas import tpu as pltpu

F32 = jnp.float32
BF16 = jnp.bfloat16

D_MODEL = 1024
DEPTH = 4
GRID_W = 64
N_MOD = 6
EPS = 1e-6
D_CONV = 512
CONV_K = 31
ATT_HEADS = 8
ATT_KV_HEADS = 2
ATT_HD = 64
ATT_GROUPS = ATT_HEADS // ATT_KV_HEADS
ROPE_THETA = 10000.0
RET_HEADS = 8
RET_DK = 64
RET_CHUNK = 128
D_FF = 2816
FFN_K = 3
N_BRANCH = 3

D_Q = ATT_HEADS * ATT_HD
D_KV = ATT_KV_HEADS * ATT_HD
D_RET = RET_HEADS * RET_DK
C_GLU = 0
C_Q = C_GLU + 2 * D_CONV
C_K = C_Q + D_Q
C_V = C_K + D_KV
C_RQ = C_V + D_KV
C_RG = C_RQ + 3 * D_RET
C_GATES = C_RG + D_RET
N_GATES = N_BRANCH * D_MODEL

TILE = 256
HALO = 16
FFN_HALO = 8
MOD_ROWS = 16
V7X_VMEM_LIMIT = 56 * 1024 * 1024


def _params(*sem):
    return pltpu.CompilerParams(dimension_semantics=sem, vmem_limit_bytes=V7X_VMEM_LIMIT)


def _norm_mod(x, g, shift, scale):
    y = x * lax.rsqrt(jnp.mean(x * x, axis=-1, keepdims=True) + EPS)
    return (y * g) * (1.0 + scale) + shift


def _sigmoid(x):
    return 1.0 / (1.0 + jnp.exp(-x))


def _silu(x):
    return x * _sigmoid(x)


def _ada_kernel(c_ref, w_ref, b_ref, o_ref):
    c = c_ref[...]
    o_ref[...] = jnp.dot(_silu(c).astype(BF16), w_ref[...].astype(BF16),
                         preferred_element_type=F32) + b_ref[...]


def _ada_mod(cvec, w_ada, b_ada):
    rows = cvec.shape[0]
    tn = 1536
    n = N_MOD * D_MODEL
    return pl.pallas_call(
        _ada_kernel,
        grid=(DEPTH, n // tn),
        in_specs=[pl.BlockSpec((rows, D_MODEL), lambda l, j: (0, 0)),
                  pl.BlockSpec((None, D_MODEL, tn), lambda l, j: (l, 0, j)),
                  pl.BlockSpec((None, 1, tn), lambda l, j: (l, 0, j))],
        out_specs=pl.BlockSpec((None, rows, tn), lambda l, j: (l, 0, j)),
        out_shape=jax.ShapeDtypeStruct((DEPTH, rows, n), F32),
        compiler_params=_params("parallel", "parallel"),
        name="ada_mod",
    )(cvec, w_ada, b_ada.reshape(DEPTH, 1, n))


def _headnorm_rope_t(xt, g, rope, n_heads):
    cr, sr, cc, sc = rope[0], rope[1], rope[2], rope[3]
    outs = []
    for h in range(n_heads):
        xh = xt[h * ATT_HD:(h + 1) * ATT_HD]
        y = xh * lax.rsqrt(jnp.mean(xh * xh, axis=0, keepdims=True) + EPS)
        y = y * g[h * ATT_HD:(h + 1) * ATT_HD]
        x1r, x2r, x1c, x2c = y[0:16], y[16:32], y[32:48], y[48:64]
        outs += [x1r * cr - x2r * sr, x2r * cr + x1r * sr,
                 x1c * cc - x2c * sc, x2c * cc + x1c * sc]
    return jnp.concatenate(outs, axis=0)


def _in_proj_kernel(x_ref, mod_ref, g1_ref, w_ref, b_ref, qg_ref, kg_ref, rope_ref,
                    u_ref, qt_ref, k_ref, vt_ref, rqkv_ref, rg_ref):
    mod = mod_ref[...]
    h = _norm_mod(x_ref[0], g1_ref[...], mod[0:1], mod[1:2]).astype(BF16)

    def proj(lo, hi):
        return jnp.dot(h, w_ref[:, lo:hi], preferred_element_type=F32) + b_ref[:, lo:hi]

    glu = proj(C_GLU, C_Q)
    u_ref[0] = glu[:, :D_CONV] * _sigmoid(glu[:, D_CONV:])

    rope = rope_ref[...]
    qt = _headnorm_rope_t(proj(C_Q, C_K).T, qg_ref[...], rope, ATT_HEADS) * (ATT_HD ** -0.5)
    qt_ref[0] = qt.astype(BF16)

    kvt = proj(C_K, C_RQ).T
    kt = _headnorm_rope_t(kvt[:D_KV], kg_ref[...], rope, ATT_KV_HEADS)
    k_ref[0] = kt.T.astype(BF16)
    vt_ref[0, 0] = kvt[D_KV:].astype(BF16)

    r = proj(C_RQ, C_RG)
    rqkv_ref[0, :, 0:D_RET] = r[:, 0:D_RET].astype(BF16)
    rqkv_ref[0, :, D_RET:2 * D_RET] = (r[:, D_RET:2 * D_RET] * (RET_DK ** -0.5)).astype(BF16)
    rqkv_ref[0, :, 2 * D_RET:] = r[:, 2 * D_RET:].astype(BF16)

    rg_ref[0] = _silu(proj(C_RG, C_GATES))


def _mod_spec(layer, n_batch, ctx_tiles, off=0):
    return pl.BlockSpec((None, None, N_MOD, D_MODEL),
                        lambda b, i: (layer, jnp.where(i + off < ctx_tiles, n_batch, b), 0, 0))


def _in_proj(layer, xa, mods, norm1_g, w_main, b_main, qg, kg, rope_t, ctx_tiles):
    nb, s, _ = xa.shape
    nt = s // TILE
    wl = lambda shape: pl.BlockSpec((None,) + shape, lambda b, i: (layer,) + (0,) * len(shape))
    out_shape = (
        jax.ShapeDtypeStruct((nb, s, D_CONV), F32),
        jax.ShapeDtypeStruct((nb, D_Q, s), BF16),
        jax.ShapeDtypeStruct((nb, s, D_KV), BF16),
        jax.ShapeDtypeStruct((nb, nt, D_KV, TILE), BF16),
        jax.ShapeDtypeStruct((nb, s, 3 * D_RET), BF16),
        jax.ShapeDtypeStruct((nb, s, D_RET), F32),
    )
    out_specs = (
        pl.BlockSpec((1, TILE, D_CONV), lambda b, i: (b, i, 0)),
        pl.BlockSpec((1, D_Q, TILE), lambda b, i: (b, 0, i)),
        pl.BlockSpec((1, TILE, D_KV), lambda b, i: (b, i, 0)),
        pl.BlockSpec((1, 1, D_KV, TILE), lambda b, i: (b, i, 0, 0)),
        pl.BlockSpec((1, TILE, 3 * D_RET), lambda b, i: (b, i, 0)),
        pl.BlockSpec((1, TILE, D_RET), lambda b, i: (b, i, 0)),
    )
    return pl.pallas_call(
        _in_proj_kernel,
        grid=(nb, nt),
        in_specs=[pl.BlockSpec((1, TILE, D_MODEL), lambda b, i: (b, i, 0)),
                  _mod_spec(layer, nb, ctx_tiles),
                  wl((1, D_MODEL)),
                  wl((D_MODEL, C_GATES)),
                  wl((1, C_GATES)),
                  wl((D_Q, 1)),
                  wl((D_KV, 1)),
                  pl.BlockSpec((4, 16, TILE), lambda b, i: (0, 0, i))],
        out_specs=out_specs,
        out_shape=out_shape,
        compiler_params=_params("parallel", "parallel"),
        name="in_proj",
    )(xa, mods, norm1_g, w_main, b_main, qg, kg, rope_t)


def _conv_kernel(u_ref, up_ref, un_ref, w_ref, b_ref, lg_ref, lb_ref, o_ref, xs_ref,
                 *, ctx_tiles, n_tiles):
    i = pl.program_id(1)
    first = (i == 0) | (i == ctx_tiles)
    last = (i == ctx_tiles - 1) | (i == n_tiles - 1)
    xs_ref[0:HALO] = jnp.where(first, 0.0, up_ref[0])
    xs_ref[HALO:HALO + TILE] = u_ref[0]
    xs_ref[HALO + TILE:] = jnp.where(last, 0.0, un_ref[0])
    rows = 32
    pad = CONV_K // 2
    for c in range(TILE // rows):
        acc = jnp.zeros((rows, D_CONV), F32)
        for k in range(CONV_K):
            start = HALO + c * rows + k - pad
            acc = acc + xs_ref[start:start + rows, :] * w_ref[k:k + 1, :]
        y = acc + b_ref[...]
        mu = jnp.mean(y, axis=-1, keepdims=True)
        yc = y - mu
        yn = yc * lax.rsqrt(jnp.mean(yc * yc, axis=-1, keepdims=True) + EPS)
        o_ref[0, c * rows:(c + 1) * rows, :] = _silu(yn * lg_ref[...] + lb_ref[...]).astype(BF16)


def _conv_module(layer, u, dw_w, dw_b, ln_g, ln_b, ctx_tiles):
    nb, s, _ = u.shape
    nt = s // TILE
    hb = TILE // HALO
    wl = lambda shape: pl.BlockSpec((None,) + shape, lambda b, i: (layer,) + (0,) * len(shape))
    return pl.pallas_call(
        functools.partial(_conv_kernel, ctx_tiles=ctx_tiles, n_tiles=nt),
        grid=(nb, nt),
        in_specs=[pl.BlockSpec((1, TILE, D_CONV), lambda b, i: (b, i, 0)),
                  pl.BlockSpec((1, HALO, D_CONV), lambda b, i: (b, jnp.maximum(i * hb - 1, 0), 0)),
                  pl.BlockSpec((1, HALO, D_CONV),
                               lambda b, i: (b, jnp.minimum((i + 1) * hb, nt * hb - 1), 0)),
                  wl((CONV_K, D_CONV)), wl((1, D_CONV)), wl((1, D_CONV)), wl((1, D_CONV))],
        out_specs=pl.BlockSpec((1, TILE, D_CONV), lambda b, i: (b, i, 0)),
        out_shape=jax.ShapeDtypeStruct((nb, s, D_CONV), BF16),
        scratch_shapes=[pltpu.VMEM((TILE + 2 * HALO, D_CONV), F32)],
        compiler_params=_params("parallel", "parallel"),
        name="conv_module",
    )(u, u, u, dw_w, dw_b, ln_g, ln_b)


def _attn_kernel(qt_ref, k_ref, vt_ref, o_ref, qpad_ref, m_ref, l_ref, acc_ref,
                 *, ctx_tiles, n_tiles):
    g = pl.program_id(1)
    i = pl.program_id(2)
    tq = qt_ref.shape[2]
    mine = (lax.broadcasted_iota(jnp.int32, (2 * ATT_HD, tq), 0) // ATT_HD) == g
    for h in range(ATT_GROUPS):
        qh = qt_ref[0, h * ATT_HD:(h + 1) * ATT_HD, :]
        q2 = jnp.concatenate([qh, qh], axis=0)
        qpad_ref[h] = jnp.where(mine, q2, jnp.zeros_like(q2))
    m_ref[...] = jnp.full(m_ref.shape, -1e30, F32)
    l_ref[...] = jnp.zeros(l_ref.shape, F32)
    acc_ref[...] = jnp.zeros(acc_ref.shape, F32)

    def body(j, carry):
        kc = k_ref[0, pl.ds(pl.multiple_of(j * TILE, TILE), TILE), :]
        vc = vt_ref[0, j]
        for h in range(ATT_GROUPS):
            s = jnp.dot(kc, qpad_ref[h], preferred_element_type=F32)
            m_prev = m_ref[h]
            m_new = jnp.maximum(m_prev, jnp.max(s, axis=0, keepdims=True))
            alpha = jnp.exp(m_prev - m_new)
            p = jnp.exp(s - m_new)
            l_ref[h] = alpha * l_ref[h] + jnp.sum(p, axis=0, keepdims=True)
            acc_ref[h] = alpha * acc_ref[h] + jnp.dot(vc, p.astype(BF16),
                                                      preferred_element_type=F32)
            m_ref[h] = m_new
        return carry

    n_kv = jnp.where(i < ctx_tiles, ctx_tiles, n_tiles)
    lax.fori_loop(0, n_kv, body, 0)
    o = jnp.concatenate([acc_ref[h] / l_ref[h] for h in range(ATT_GROUPS)], axis=0)
    o_ref[0] = o.T.astype(BF16)


def _attention(qt, k, vt, ctx_tiles):
    nb, _, s = qt.shape
    nt = s // TILE
    gw = ATT_GROUPS * ATT_HD
    return pl.pallas_call(
        functools.partial(_attn_kernel, ctx_tiles=ctx_tiles, n_tiles=nt),
        grid=(nb, ATT_KV_HEADS, nt),
        in_specs=[pl.BlockSpec((1, gw, TILE), lambda b, g, i: (b, g, i)),
                  pl.BlockSpec((1, s, D_KV), lambda b, g, i: (b, 0, 0)),
                  pl.BlockSpec((1, nt, ATT_HD, TILE), lambda b, g, i: (b, 0, g, 0))],
        out_specs=pl.BlockSpec((1, TILE, gw), lambda b, g, i: (b, i, g)),
        out_shape=jax.ShapeDtypeStruct((nb, s, D_Q), BF16),
        scratch_shapes=[pltpu.VMEM((ATT_GROUPS, 2 * ATT_HD, TILE), BF16),
                        pltpu.VMEM((ATT_GROUPS, 1, TILE), F32),
                        pltpu.VMEM((ATT_GROUPS, 1, TILE), F32),
                        pltpu.VMEM((ATT_GROUPS, ATT_HD, TILE), F32)],
        compiler_params=_params("parallel", "parallel", "arbitrary"),
        name="attention",
    )(qt, k, vt)


def _log_sigmoid(x):
    return jnp.minimum(x, 0.0) - jnp.log(1.0 + jnp.exp(-jnp.abs(x)))


def _seg_mean(x, m):
    hi = x.astype(BF16)
    lo = (x - hi.astype(F32)).astype(BF16)
    return (jnp.dot(hi, m, preferred_element_type=F32)
            + jnp.dot(lo, m, preferred_element_type=F32))


def _ret_kernel(*refs, rev, final):
    if final:
        (q_ref, k_ref, v_ref, lgh_ref, lgl_ref, of_ref, rg_ref, gn_ref,
         o_ref, dec_ref, qd_ref, kd_ref, cd_ref, st_ref) = refs
    else:
        (q_ref, k_ref, v_ref, lgh_ref, lgl_ref,
         o_ref, dec_ref, qd_ref, kd_ref, cd_ref, st_ref) = refs
    c = RET_CHUNK
    pw = 2 * RET_DK

    @pl.when(pl.program_id(1) == 0)
    def _():
        ti = lax.broadcasted_iota(jnp.int32, (c, c), 0).astype(F32)
        tj = lax.broadcasted_iota(jnp.int32, (c, c), 1).astype(F32)
        rel = (tj - ti) if rev else (ti - tj)
        keep = rel >= 0
        for h in range(RET_HEADS):
            lg = _log_sigmoid(lgh_ref[h])
            dec_ref[h] = jnp.where(keep, jnp.exp(jnp.where(keep, rel, 0.0) * lg), 0.0)
        lgl = _log_sigmoid(lgl_ref[...])
        t = lax.broadcasted_iota(jnp.int32, (c, D_RET), 0).astype(F32)
        qd_ref[...] = jnp.exp(((c - t) if rev else (t + 1.0)) * lgl)
        kd_ref[...] = jnp.exp((t if rev else (c - 1.0 - t)) * lgl)
        cd_ref[...] = jnp.exp(float(c) * lgl)
        st_ref[...] = jnp.zeros(st_ref.shape, F32)

    lane = lax.broadcasted_iota(jnp.int32, (c, pw), 1)
    row = lax.broadcasted_iota(jnp.int32, (pw, pw), 0)
    col = lax.broadcasted_iota(jnp.int32, (pw, pw), 1)
    same_head = (row >= RET_DK) == (col >= RET_DK)
    avg = jnp.where(same_head, 1.0 / RET_DK, 0.0).astype(BF16)
    nt_dims = (((1,), (1,)), ((), ()))

    for p in range(RET_HEADS // 2):
        sl = slice(p * pw, (p + 1) * pw)
        qp, kp, vp = q_ref[0, :, sl], k_ref[0, :, sl], v_ref[0, :, sl]
        st = st_ref[p]
        o = jnp.dot((qp.astype(F32) * qd_ref[:, sl]).astype(BF16), st.astype(BF16),
                    preferred_element_type=F32)
        for hh in range(2):
            msk = (lane >= RET_DK) if hh else (lane < RET_DK)
            qm = jnp.where(msk, qp, jnp.zeros_like(qp))
            inner = lax.dot_general(qm, kp, nt_dims, preferred_element_type=F32)
            a = (inner * dec_ref[2 * p + hh]).astype(BF16)
            o = o + jnp.dot(a, jnp.where(msk, vp, jnp.zeros_like(vp)),
                            preferred_element_type=F32)
        kdt = (kp.astype(F32) * kd_ref[:, sl]).T.astype(BF16)
        upd = jnp.dot(kdt, vp, preferred_element_type=F32)
        st_ref[p] = st * cd_ref[:, sl] + jnp.where(same_head, upd, 0.0)
        if final:
            o = o + of_ref[0, :, sl]
            oc = o - _seg_mean(o, avg)
            y = oc * lax.rsqrt(_seg_mean(oc * oc, avg) + EPS) * gn_ref[:, sl]
            o_ref[0, :, sl] = (y * rg_ref[0, :, sl]).astype(o_ref.dtype)
        else:
            o_ref[0, :, sl] = o


def _retention(layer, rqkv, lg_head, lg_lane, o_fwd=None, rg=None, gn_g=None, *, rev, ctx_chunks):
    nb, s, _ = rqkv.shape
    nch = s // RET_CHUNK
    final = o_fwd is not None
    d = 1 if rev else 0
    if rev:
        cmap = lambda j: jnp.where(j < ctx_chunks, ctx_chunks - 1 - j, nch - 1 - (j - ctx_chunks))
    else:
        cmap = lambda j: j
    tok = lambda col: pl.BlockSpec((1, RET_CHUNK, D_RET), lambda b, j: (b, cmap(j), col))
    in_specs = [tok(0), tok(1), tok(2),
                pl.BlockSpec((None, None, RET_HEADS, 1, 2 * RET_DK), lambda b, j: (layer, d, 0, 0, 0)),
                pl.BlockSpec((None, None, 1, D_RET), lambda b, j: (layer, d, 0, 0))]
    args = [rqkv, rqkv, rqkv, lg_head, lg_lane]
    if final:
        in_specs += [tok(0), tok(0), pl.BlockSpec((None, 1, D_RET), lambda b, j: (layer, 0, 0))]
        args += [o_fwd, rg, gn_g]
    return pl.pallas_call(
        functools.partial(_ret_kernel, rev=rev, final=final),
        grid=(nb, nch),
        in_specs=in_specs,
        out_specs=tok(0),
        out_shape=jax.ShapeDtypeStruct((nb, s, D_RET), BF16 if final else F32),
        scratch_shapes=[pltpu.VMEM((RET_HEADS, RET_CHUNK, RET_CHUNK), F32),
                        pltpu.VMEM((RET_CHUNK, D_RET), F32),
                        pltpu.VMEM((RET_CHUNK, D_RET), F32),
                        pltpu.VMEM((1, D_RET), F32),
                        pltpu.VMEM((RET_HEADS // 2, 2 * RET_DK, 2 * RET_DK), F32)],
        compiler_params=_params("parallel", "arbitrary"),
        name="retention_bwd" if rev else "retention_fwd",
    )(*args)


def _merge_kernel(x_ref, mod_ref, g1_ref, wg_ref, bg_ref, a_ref, att_ref, ret_ref,
                  wpa_ref, wpb_ref, wpc_ref, wo_ref, o_ref):
    mod = mod_ref[...]
    x = x_ref[0]
    h = _norm_mod(x, g1_ref[...], mod[0:1], mod[1:2]).astype(BF16)
    merged = None
    for n, (br_ref, w_ref) in enumerate(((a_ref, wpa_ref), (att_ref, wpb_ref), (ret_ref, wpc_ref))):
        sl = slice(n * D_MODEL, (n + 1) * D_MODEL)
        gate = _sigmoid(jnp.dot(h, wg_ref[:, sl], preferred_element_type=F32) + bg_ref[:, sl])
        term = gate * jnp.dot(br_ref[0], w_ref[...], preferred_element_type=F32)
        merged = term if merged is None else merged + term
    y = jnp.dot(merged.astype(BF16), wo_ref[...], preferred_element_type=F32)
    o_ref[0] = x + mod[2:3] * y


def _merge(layer, xa, mods, norm1_g, w_gates, b_gates, a, att, ret, w_pa, w_pb, w_pc, w_out,
           ctx_tiles, skip_ctx):
    nb, s, _ = xa.shape
    off = ctx_tiles if skip_ctx else 0
    nt = s // TILE - off
    wl = lambda shape: pl.BlockSpec((None,) + shape, lambda b, i: (layer,) + (0,) * len(shape))
    tok = lambda w: pl.BlockSpec((1, TILE, w), lambda b, i: (b, i + off, 0))
    return pl.pallas_call(
        _merge_kernel,
        grid=(nb, nt),
        in_specs=[tok(D_MODEL), _mod_spec(layer, nb, ctx_tiles, off), wl((1, D_MODEL)),
                  wl((D_MODEL, N_GATES)), wl((1, N_GATES)),
                  tok(D_CONV), tok(D_Q), tok(D_RET),
                  wl((D_CONV, D_MODEL)), wl((D_Q, D_MODEL)), wl((D_RET, D_MODEL)),
                  wl((D_MODEL, D_MODEL))],
        out_specs=tok(D_MODEL),
        out_shape=jax.ShapeDtypeStruct(xa.shape, F32),
        compiler_params=_params("parallel", "parallel"),
        name="merge",
    )(xa, mods, norm1_g, w_gates, b_gates, a, att, ret, w_pa, w_pb, w_pc, w_out)


def _ffn_kernel(x_ref, xp_ref, xn_ref, mod_ref, g2_ref, wu_ref, dw_ref, db_ref, wd_ref, gf_ref,
                o_ref, gs_ref, *, ctx_tiles, n_tiles, off, final):
    i = pl.program_id(1) + off
    first = (i == 0) | (i == ctx_tiles)
    last = (i == ctx_tiles - 1) | (i == n_tiles - 1)
    mod = mod_ref[...]
    x = x_ref[0]
    xe = jnp.concatenate([xp_ref[0], x, xn_ref[0]], axis=0)
    he = _norm_mod(xe, g2_ref[...], mod[3:4], mod[4:5]).astype(BF16)
    gt = jnp.dot(he, wu_ref[:, :D_FF], preferred_element_type=F32)
    gs_ref[0:FFN_HALO] = jnp.where(first, 0.0, gt[0:FFN_HALO])
    gs_ref[FFN_HALO:FFN_HALO + TILE] = gt[FFN_HALO:FFN_HALO + TILE]
    gs_ref[FFN_HALO + TILE:] = jnp.where(last, 0.0, gt[FFN_HALO + TILE:])
    val = jnp.dot(he[FFN_HALO:FFN_HALO + TILE], wu_ref[:, D_FF:], preferred_element_type=F32)
    conv = db_ref[...]
    for k in range(FFN_K):
        start = FFN_HALO + k - FFN_K // 2
        conv = conv + gs_ref[start:start + TILE, :] * dw_ref[k:k + 1, :]
    act = (_silu(conv) * val).astype(BF16)
    y = x + mod[5:6] * jnp.dot(act, wd_ref[...], preferred_element_type=F32)
    if final:
        y = y * lax.rsqrt(jnp.mean(y * y, axis=-1, keepdims=True) + EPS) * gf_ref[...]
    o_ref[0] = y


def _ffn(layer, xa, mods, norm2_g, w_up, dw_w, dw_b, w_down, final_g, ctx_tiles, final):
    nb, s, _ = xa.shape
    n_tiles = s // TILE
    off = ctx_tiles if final else 0
    nt = n_tiles - off
    hb = TILE // FFN_HALO
    wl = lambda shape: pl.BlockSpec((None,) + shape, lambda b, i: (layer,) + (0,) * len(shape))
    out_rows = s - off * TILE
    return pl.pallas_call(
        functools.partial(_ffn_kernel, ctx_tiles=ctx_tiles, n_tiles=n_tiles, off=off, final=final),
        grid=(nb, nt),
        in_specs=[pl.BlockSpec((1, TILE, D_MODEL), lambda b, i: (b, i + off, 0)),
                  pl.BlockSpec((1, FFN_HALO, D_MODEL),
                               lambda b, i: (b, jnp.maximum((i + off) * hb - 1, 0), 0)),
                  pl.BlockSpec((1, FFN_HALO, D_MODEL),
                               lambda b, i: (b, jnp.minimum((i + off + 1) * hb, n_tiles * hb - 1), 0)),
                  _mod_spec(layer, nb, ctx_tiles, off), wl((1, D_MODEL)),
                  wl((D_MODEL, 2 * D_FF)), wl((FFN_K, D_FF)), wl((1, D_FF)), wl((D_FF, D_MODEL)),
                  pl.BlockSpec((1, D_MODEL), lambda b, i: (0, 0))],
        out_specs=pl.BlockSpec((1, TILE, D_MODEL), lambda b, i: (b, i, 0)),
        out_shape=jax.ShapeDtypeStruct((nb, out_rows, D_MODEL), F32),
        scratch_shapes=[pltpu.VMEM((TILE + 2 * FFN_HALO, D_FF), F32)],
        compiler_params=_params("parallel", "parallel"),
        name="ffn_final" if final else "ffn",
    )(xa, xa, xa, mods, norm2_g, w_up, dw_w, dw_b, w_down, final_g)


def _rope_tables_t(ctx_len, seq_len):
    rows = seq_len // GRID_W
    row = jnp.repeat(jnp.arange(rows, dtype=F32), GRID_W)
    col = jnp.tile(jnp.arange(GRID_W, dtype=F32), rows)
    n_freq = ATT_HD // 4
    inv = ROPE_THETA ** (-jnp.arange(n_freq, dtype=F32) / n_freq)
    ang_r = inv[:, None] * row[None, :]
    ang_c = inv[:, None] * col[None, :]
    tabs = jnp.stack([jnp.cos(ang_r), jnp.sin(ang_r), jnp.cos(ang_c), jnp.sin(ang_c)])
    ident = jnp.stack([jnp.ones((n_freq, ctx_len), F32), jnp.zeros((n_freq, ctx_len), F32)] * 2)
    return jnp.concatenate([ident, tabs], axis=2)


def kernel(x, c, ctx, c_ctx, w_ada, b_ada, norm1_g, w_in, b_in, conv_dw_w, conv_dw_b, conv_ln_g, conv_ln_b, q_norm_g, k_norm_g, ret_decay_logit, ret_gn_g, w_pa, w_pb, w_pc, w_out, norm2_g, w_up, ffn_dw_w, ffn_dw_b, w_down, final_norm_g):
    nb, seq_len, _ = x.shape
    ctx_len = ctx.shape[1]
    assert ctx_len % TILE == 0 and seq_len % TILE == 0 and seq_len % GRID_W == 0
    ctx_tiles = ctx_len // TILE
    ctx_chunks = ctx_len // RET_CHUNK

    xa = jnp.concatenate([ctx, x], axis=1)
    cvec = jnp.zeros((MOD_ROWS, D_MODEL), F32).at[:nb].set(c).at[nb].set(c_ctx)
    rope_t = _rope_tables_t(ctx_len, seq_len)
    row3 = lambda a: a.reshape(DEPTH, 1, a.shape[-1])
    w_main = w_in[:, :, :C_GATES].astype(BF16)
    w_gates = w_in[:, :, C_GATES:].astype(BF16)
    b_main, b_gates = row3(b_in[:, :C_GATES]), row3(b_in[:, C_GATES:])
    qg = jnp.tile(q_norm_g, (1, ATT_HEADS))[:, :, None]
    kg = jnp.tile(k_norm_g, (1, ATT_KV_HEADS))[:, :, None]
    lg_head = jnp.broadcast_to(ret_decay_logit[:, :, :, None, None],
                               (DEPTH, 2, RET_HEADS, 1, 2 * RET_DK)).astype(F32)
    lg_lane = jnp.repeat(ret_decay_logit.astype(F32), RET_DK, axis=-1)[:, :, None, :]
    w_pa_b, w_pb_b, w_pc_b = w_pa.astype(BF16), w_pb.astype(BF16), w_pc.astype(BF16)
    w_out_b, w_up_b, w_down_b = w_out.astype(BF16), w_up.astype(BF16), w_down.astype(BF16)
    n1, n2 = row3(norm1_g), row3(norm2_g)
    cdb, clg, clb = row3(conv_dw_b), row3(conv_ln_g), row3(conv_ln_b)
    fdb, gng = row3(ffn_dw_b), row3(ret_gn_g)
    final_g = final_norm_g.reshape(1, D_MODEL)

    mods = _ada_mod(cvec, w_ada, b_ada).reshape(DEPTH, MOD_ROWS, N_MOD, D_MODEL)

    for layer in range(DEPTH):
        last = layer == DEPTH - 1
        u, qt, k, vt, rqkv, rg = _in_proj(layer, xa, mods, n1, w_main, b_main, qg, kg, rope_t, ctx_tiles)
        a = _conv_module(layer, u, conv_dw_w, cdb, clg, clb, ctx_tiles)
        att = _attention(qt, k, vt, ctx_tiles)
        o_fwd = _retention(layer, rqkv, lg_head, lg_lane, rev=False, ctx_chunks=ctx_chunks)
        ret = _retention(layer, rqkv, lg_head, lg_lane, o_fwd, rg, gng, rev=True, ctx_chunks=ctx_chunks)
        xm = _merge(layer, xa, mods, n1, w_gates, b_gates, a, att, ret,
                    w_pa_b, w_pb_b, w_pc_b, w_out_b, ctx_tiles, skip_ctx=False)
        xa = _ffn(layer, xm, mods, n2, w_up_b, ffn_dw_w, fdb, w_down_b, final_g, ctx_tiles, final=last)
    return xa
```

```python
import functools

import jax
import jax.numpy as jnp
from jax import lax
from jax.experimental import pallas as pl
from jax.experimental.pallas import tpu as pltpu

F32 = jnp.float32
BF16 = jnp.bfloat16

D_MODEL = 1024
DEPTH = 4
GRID_W = 64
N_MOD = 6
EPS = 1e-6
D_CONV = 512
CONV_K = 31
ATT_HEADS = 8
ATT_KV_HEADS = 2
ATT_HD = 64
ATT_GROUPS = ATT_HEADS // ATT_KV_HEADS
ROPE_THETA = 10000.0
RET_HEADS = 8
RET_DK = 64
RET_CHUNK = 128
D_FF = 2816
FFN_K = 3
N_BRANCH = 3

D_Q = ATT_HEADS * ATT_HD
D_KV = ATT_KV_HEADS * ATT_HD
D_RET = RET_HEADS * RET_DK
C_GLU = 0
C_Q = C_GLU + 2 * D_CONV
C_K = C_Q + D_Q
C_V = C_K + D_KV
C_RQ = C_V + D_KV
C_RG = C_RQ + 3 * D_RET
C_GATES = C_RG + D_RET
N_GATES = N_BRANCH * D_MODEL

TILE = 256
HALO = 16
FFN_HALO = 8
V_ROWS = ATT_HD + 16
LOG2E = 1.4426950408889634
MOD_ROWS = 16
V7X_VMEM_LIMIT = 56 * 1024 * 1024


def _params(*sem):
    return pltpu.CompilerParams(dimension_semantics=sem, vmem_limit_bytes=V7X_VMEM_LIMIT)


def _norm_mod(x, g, shift, scale):
    y = x * lax.rsqrt(jnp.mean(x * x, axis=-1, keepdims=True) + EPS)
    return (y * g) * (1.0 + scale) + shift


def _sigmoid(x):
    return 1.0 / (1.0 + jnp.exp(-x))


def _silu(x):
    return x * _sigmoid(x)


def _ada_kernel(c_ref, w_ref, b_ref, o_ref):
    c = c_ref[...]
    o_ref[...] = jnp.dot(_silu(c).astype(BF16), w_ref[...].astype(BF16),
                         preferred_element_type=F32) + b_ref[...]


def _ada_mod(cvec, w_ada, b_ada):
    rows = cvec.shape[0]
    tn = 1536
    n = N_MOD * D_MODEL
    return pl.pallas_call(
        _ada_kernel,
        grid=(DEPTH, n // tn),
        in_specs=[pl.BlockSpec((rows, D_MODEL), lambda l, j: (0, 0)),
                  pl.BlockSpec((None, D_MODEL, tn), lambda l, j: (l, 0, j)),
                  pl.BlockSpec((None, 1, tn), lambda l, j: (l, 0, j))],
        out_specs=pl.BlockSpec((None, rows, tn), lambda l, j: (l, 0, j)),
        out_shape=jax.ShapeDtypeStruct((DEPTH, rows, n), F32),
        compiler_params=_params("parallel", "parallel"),
        name="ada_mod",
    )(cvec, w_ada, b_ada.reshape(DEPTH, 1, n))


def _headnorm_rope_t(xt, g, rope, n_heads):
    cr, sr, cc, sc = rope[0], rope[1], rope[2], rope[3]
    outs = []
    for h in range(n_heads):
        xh = xt[h * ATT_HD:(h + 1) * ATT_HD]
        y = xh * lax.rsqrt(jnp.mean(xh * xh, axis=0, keepdims=True) + EPS)
        y = y * g[h * ATT_HD:(h + 1) * ATT_HD]
        x1r, x2r, x1c, x2c = y[0:16], y[16:32], y[32:48], y[48:64]
        outs += [x1r * cr - x2r * sr, x2r * cr + x1r * sr,
                 x1c * cc - x2c * sc, x2c * cc + x1c * sc]
    return jnp.concatenate(outs, axis=0)


def _in_proj_kernel(x_ref, mod_ref, g1_ref, w_ref, b_ref, qg_ref, kg_ref, rope_ref,
                    u_ref, qt_ref, k_ref, vt_ref, rqkv_ref, rg_ref):
    mod = mod_ref[...]
    h = _norm_mod(x_ref[0], g1_ref[...], mod[0:1], mod[1:2]).astype(BF16)

    def proj(lo, hi):
        return jnp.dot(h, w_ref[:, lo:hi], preferred_element_type=F32) + b_ref[:, lo:hi]

    glu = proj(C_GLU, C_Q)
    u_ref[0] = glu[:, :D_CONV] * _sigmoid(glu[:, D_CONV:])

    rope = rope_ref[...]
    qt = _headnorm_rope_t(proj(C_Q, C_K).T, qg_ref[...], rope, ATT_HEADS) * (ATT_HD ** -0.5 * LOG2E)
    qt_ref[0] = qt.astype(BF16)

    kvt = proj(C_K, C_RQ).T
    kt = _headnorm_rope_t(kvt[:D_KV], kg_ref[...], rope, ATT_KV_HEADS)
    k_ref[0] = kt.T.astype(BF16)
    ones = jnp.ones((V_ROWS - ATT_HD, kvt.shape[1]), BF16)
    for g in range(ATT_KV_HEADS):
        vt_ref[0, 0, g * V_ROWS:g * V_ROWS + ATT_HD] = kvt[D_KV + g * ATT_HD:D_KV + (g + 1) * ATT_HD].astype(BF16)
        vt_ref[0, 0, g * V_ROWS + ATT_HD:(g + 1) * V_ROWS] = ones

    r = proj(C_RQ, C_RG)
    rqkv_ref[0, :, 0:D_RET] = r[:, 0:D_RET].astype(BF16)
    rqkv_ref[0, :, D_RET:2 * D_RET] = (r[:, D_RET:2 * D_RET] * (RET_DK ** -0.5)).astype(BF16)
    rqkv_ref[0, :, 2 * D_RET:] = r[:, 2 * D_RET:].astype(BF16)

    rg_ref[0] = _silu(proj(C_RG, C_GATES))


def _mod_spec(layer, n_batch, ctx_tiles, off=0):
    return pl.BlockSpec((None, None, N_MOD, D_MODEL),
                        lambda b, i: (layer, jnp.where(i + off < ctx_tiles, n_batch, b), 0, 0))


def _in_proj(layer, xa, mods, norm1_g, w_main, b_main, qg, kg, rope_t, ctx_tiles):
    nb, s, _ = xa.shape
    nt = s // TILE
    wl = lambda shape: pl.BlockSpec((None,) + shape, lambda b, i: (layer,) + (0,) * len(shape))
    out_shape = (
        jax.ShapeDtypeStruct((nb, s, D_CONV), F32),
        jax.ShapeDtypeStruct((nb, D_Q, s), BF16),
        jax.ShapeDtypeStruct((nb, s, D_KV), BF16),
        jax.ShapeDtypeStruct((nb, nt, ATT_KV_HEADS * V_ROWS, TILE), BF16),
        jax.ShapeDtypeStruct((nb, s, 3 * D_RET), BF16),
        jax.ShapeDtypeStruct((nb, s, D_RET), F32),
    )
    out_specs = (
        pl.BlockSpec((1, TILE, D_CONV), lambda b, i: (b, i, 0)),
        pl.BlockSpec((1, D_Q, TILE), lambda b, i: (b, 0, i)),
        pl.BlockSpec((1, TILE, D_KV), lambda b, i: (b, i, 0)),
        pl.BlockSpec((1, 1, ATT_KV_HEADS * V_ROWS, TILE), lambda b, i: (b, i, 0, 0)),
        pl.BlockSpec((1, TILE, 3 * D_RET), lambda b, i: (b, i, 0)),
        pl.BlockSpec((1, TILE, D_RET), lambda b, i: (b, i, 0)),
    )
    return pl.pallas_call(
        _in_proj_kernel,
        grid=(nb, nt),
        in_specs=[pl.BlockSpec((1, TILE, D_MODEL), lambda b, i: (b, i, 0)),
                  _mod_spec(layer, nb, ctx_tiles),
                  wl((1, D_MODEL)),
                  wl((D_MODEL, C_GATES)),
                  wl((1, C_GATES)),
                  wl((D_Q, 1)),
                  wl((D_KV, 1)),
                  pl.BlockSpec((4, 16, TILE), lambda b, i: (0, 0, i))],
        out_specs=out_specs,
        out_shape=out_shape,
        compiler_params=_params("parallel", "parallel"),
        name="in_proj",
    )(xa, mods, norm1_g, w_main, b_main, qg, kg, rope_t)


def _conv_kernel(u_ref, up_ref, un_ref, w_ref, b_ref, lg_ref, lb_ref, o_ref, xs_ref,
                 *, ctx_tiles, n_tiles):
    i = pl.program_id(1)
    first = (i == 0) | (i == ctx_tiles)
    last = (i == ctx_tiles - 1) | (i == n_tiles - 1)
    xs_ref[0:HALO] = jnp.where(first, 0.0, up_ref[0])
    xs_ref[HALO:HALO + TILE] = u_ref[0]
    xs_ref[HALO + TILE:] = jnp.where(last, 0.0, un_ref[0])
    rows = 32
    pad = CONV_K // 2
    for c in range(TILE // rows):
        acc = jnp.zeros((rows, D_CONV), F32)
        for k in range(CONV_K):
            start = HALO + c * rows + k - pad
            acc = acc + xs_ref[start:start + rows, :] * w_ref[k:k + 1, :]
        y = acc + b_ref[...]
        mu = jnp.mean(y, axis=-1, keepdims=True)
        yc = y - mu
        yn = yc * lax.rsqrt(jnp.mean(yc * yc, axis=-1, keepdims=True) + EPS)
        o_ref[0, c * rows:(c + 1) * rows, :] = _silu(yn * lg_ref[...] + lb_ref[...]).astype(BF16)


def _conv_module(layer, u, dw_w, dw_b, ln_g, ln_b, ctx_tiles):
    nb, s, _ = u.shape
    nt = s // TILE
    hb = TILE // HALO
    wl = lambda shape: pl.BlockSpec((None,) + shape, lambda b, i: (layer,) + (0,) * len(shape))
    return pl.pallas_call(
        functools.partial(_conv_kernel, ctx_tiles=ctx_tiles, n_tiles=nt),
        grid=(nb, nt),
        in_specs=[pl.BlockSpec((1, TILE, D_CONV), lambda b, i: (b, i, 0)),
                  pl.BlockSpec((1, HALO, D_CONV), lambda b, i: (b, jnp.maximum(i * hb - 1, 0), 0)),
                  pl.BlockSpec((1, HALO, D_CONV),
                               lambda b, i: (b, jnp.minimum((i + 1) * hb, nt * hb - 1), 0)),
                  wl((CONV_K, D_CONV)), wl((1, D_CONV)), wl((1, D_CONV)), wl((1, D_CONV))],
        out_specs=pl.BlockSpec((1, TILE, D_CONV), lambda b, i: (b, i, 0)),
        out_shape=jax.ShapeDtypeStruct((nb, s, D_CONV), BF16),
        scratch_shapes=[pltpu.VMEM((TILE + 2 * HALO, D_CONV), F32)],
        compiler_params=_params("parallel", "parallel"),
        name="conv_module",
    )(u, u, u, dw_w, dw_b, ln_g, ln_b)


def _attn_kernel(qt_ref, k_ref, vt_ref, o_ref, qpad_ref, s_ref, m_ref, acc_ref,
                 *, ctx_tiles, n_tiles):
    g = pl.program_id(1)
    i = pl.program_id(2)
    tq = qt_ref.shape[2]
    mine = (lax.broadcasted_iota(jnp.int32, (2 * ATT_HD, tq), 0) // ATT_HD) == g
    for h in range(ATT_GROUPS):
        qh = qt_ref[0, h * ATT_HD:(h + 1) * ATT_HD, :]
        q2 = jnp.concatenate([qh, qh], axis=0)
        qpad_ref[h] = jnp.where(mine, q2, jnp.zeros_like(q2))
    m_ref[...] = jnp.full(m_ref.shape, -1e30, F32)
    acc_ref[...] = jnp.zeros(acc_ref.shape, F32)

    def scores(j, slot):
        kc = k_ref[0, pl.ds(pl.multiple_of(j * TILE, TILE), TILE), :]
        for h in range(ATT_GROUPS):
            s_ref[slot, h] = jnp.dot(kc, qpad_ref[h], preferred_element_type=F32)

    def update(j, slot):
        vc = vt_ref[0, j]
        for h in range(ATT_GROUPS):
            s = s_ref[slot, h]
            m_prev = m_ref[h]
            m_new = jnp.maximum(m_prev, jnp.max(s, axis=0, keepdims=True))
            p = jnp.exp2(s - m_new).astype(BF16)
            acc_ref[h] = jnp.exp2(m_prev - m_new) * acc_ref[h] + jnp.dot(
                vc, p, preferred_element_type=F32)
            m_ref[h] = m_new

    def pair(jj, carry):
        j = 2 * jj
        scores(j + 1, 1)
        update(j, 0)
        scores(j + 2, 0)
        update(j + 1, 1)
        return carry

    is_ctx = i < ctx_tiles
    scores(0, 0)
    lax.fori_loop(0, jnp.where(is_ctx, (ctx_tiles - 1) // 2, (n_tiles - 1) // 2), pair, 0)
    update(jnp.where(is_ctx, ctx_tiles - 1, n_tiles - 1), 0)
    o = jnp.concatenate([acc_ref[h, :ATT_HD] / acc_ref[h, ATT_HD:ATT_HD + 1]
                         for h in range(ATT_GROUPS)], axis=0)
    o_ref[0] = o.T.astype(BF16)


def _attention(qt, k, vt, ctx_tiles):
    nb, _, s = qt.shape
    nt = s // TILE
    assert nt % 2 == 1 and ctx_tiles % 2 == 1
    gw = ATT_GROUPS * ATT_HD
    return pl.pallas_call(
        functools.partial(_attn_kernel, ctx_tiles=ctx_tiles, n_tiles=nt),
        grid=(nb, ATT_KV_HEADS, nt),
        in_specs=[pl.BlockSpec((1, gw, TILE), lambda b, g, i: (b, g, i)),
                  pl.BlockSpec((1, s, D_KV), lambda b, g, i: (b, 0, 0)),
                  pl.BlockSpec((1, nt, V_ROWS, TILE), lambda b, g, i: (b, 0, g, 0))],
        out_specs=pl.BlockSpec((1, TILE, gw), lambda b, g, i: (b, i, g)),
        out_shape=jax.ShapeDtypeStruct((nb, s, D_Q), BF16),
        scratch_shapes=[pltpu.VMEM((ATT_GROUPS, 2 * ATT_HD, TILE), BF16),
                        pltpu.VMEM((2, ATT_GROUPS, TILE, TILE), F32),
                        pltpu.VMEM((ATT_GROUPS, 1, TILE), F32),
                        pltpu.VMEM((ATT_GROUPS, V_ROWS, TILE), F32)],
        compiler_params=_params("parallel", "parallel", "arbitrary"),
        name="attention",
    )(qt, k, vt)


def _log_sigmoid(x):
    return jnp.minimum(x, 0.0) - jnp.log(1.0 + jnp.exp(-jnp.abs(x)))


def _seg_mean(x, m):
    hi = x.astype(BF16)
    lo = (x - hi.astype(F32)).astype(BF16)
    return (jnp.dot(hi, m, preferred_element_type=F32)
            + jnp.dot(lo, m, preferred_element_type=F32))


def _ret_kernel(*refs, rev, final):
    if final:
        (q_ref, k_ref, v_ref, lgh_ref, lgl_ref, of_ref, rg_ref, gn_ref,
         o_ref, dec_ref, qd_ref, kd_ref, cd_ref, st_ref) = refs
    else:
        (q_ref, k_ref, v_ref, lgh_ref, lgl_ref,
         o_ref, dec_ref, qd_ref, kd_ref, cd_ref, st_ref) = refs
    c = RET_CHUNK
    pw = 2 * RET_DK

    @pl.when(pl.program_id(1) == 0)
    def _():
        ti = lax.broadcasted_iota(jnp.int32, (c, c), 0).astype(F32)
        tj = lax.broadcasted_iota(jnp.int32, (c, c), 1).astype(F32)
        rel = (tj - ti) if rev else (ti - tj)
        keep = rel >= 0
        for h in range(RET_HEADS):
            lg = _log_sigmoid(lgh_ref[h])
            dec_ref[h] = jnp.where(keep, jnp.exp(jnp.where(keep, rel, 0.0) * lg), 0.0)
        lgl = _log_sigmoid(lgl_ref[...])
        t = lax.broadcasted_iota(jnp.int32, (c, D_RET), 0).astype(F32)
        qd_ref[...] = jnp.exp(((c - t) if rev else (t + 1.0)) * lgl)
        kd_ref[...] = jnp.exp((t if rev else (c - 1.0 - t)) * lgl)
        cd_ref[...] = jnp.exp(float(c) * lgl)
        st_ref[...] = jnp.zeros(st_ref.shape, F32)

    lane = lax.broadcasted_iota(jnp.int32, (c, pw), 1)
    row = lax.broadcasted_iota(jnp.int32, (pw, pw), 0)
    col = lax.broadcasted_iota(jnp.int32, (pw, pw), 1)
    same_head = (row >= RET_DK) == (col >= RET_DK)
    avg = jnp.where(same_head, 1.0 / RET_DK, 0.0).astype(BF16)
    nt_dims = (((1,), (1,)), ((), ()))

    for p in range(RET_HEADS // 2):
        sl = slice(p * pw, (p + 1) * pw)
        qp, kp, vp = q_ref[0, :, sl], k_ref[0, :, sl], v_ref[0, :, sl]
        st = st_ref[p]
        o = jnp.dot((qp.astype(F32) * qd_ref[:, sl]).astype(BF16), st.astype(BF16),
                    preferred_element_type=F32)
        for hh in range(2):
            msk = (lane >= RET_DK) if hh else (lane < RET_DK)
            qm = jnp.where(msk, qp, jnp.zeros_like(qp))
            inner = lax.dot_general(qm, kp, nt_dims, preferred_element_type=F32)
            a = (inner * dec_ref[2 * p + hh]).astype(BF16)
            o = o + jnp.dot(a, jnp.where(msk, vp, jnp.zeros_like(vp)),
                            preferred_element_type=F32)
        kdt = (kp.astype(F32) * kd_ref[:, sl]).T.astype(BF16)
        upd = jnp.dot(kdt, vp, preferred_element_type=F32)
        st_ref[p] = st * cd_ref[:, sl] + jnp.where(same_head, upd, 0.0)
        if final:
            o = o + of_ref[0, :, sl]
            oc = o - _seg_mean(o, avg)
            y = oc * lax.rsqrt(_seg_mean(oc * oc, avg) + EPS) * gn_ref[:, sl]
            o_ref[0, :, sl] = (y * rg_ref[0, :, sl]).astype(o_ref.dtype)
        else:
            o_ref[0, :, sl] = o


def _retention(layer, rqkv, lg_head, lg_lane, o_fwd=None, rg=None, gn_g=None, *, rev, ctx_chunks):
    nb, s, _ = rqkv.shape
    nch = s // RET_CHUNK
    final = o_fwd is not None
    d = 1 if rev else 0
    if rev:
        cmap = lambda j: jnp.where(j < ctx_chunks, ctx_chunks - 1 - j, nch - 1 - (j - ctx_chunks))
    else:
        cmap = lambda j: j
    tok = lambda col: pl.BlockSpec((1, RET_CHUNK, D_RET), lambda b, j: (b, cmap(j), col))
    in_specs = [tok(0), tok(1), tok(2),
                pl.BlockSpec((None, None, RET_HEADS, 1, 2 * RET_DK), lambda b, j: (layer, d, 0, 0, 0)),
                pl.BlockSpec((None, None, 1, D_RET), lambda b, j: (layer, d, 0, 0))]
    args = [rqkv, rqkv, rqkv, lg_head, lg_lane]
    if final:
        in_specs += [tok(0), tok(0), pl.BlockSpec((None, 1, D_RET), lambda b, j: (layer, 0, 0))]
        args += [o_fwd, rg, gn_g]
    return pl.pallas_call(
        functools.partial(_ret_kernel, rev=rev, final=final),
        grid=(nb, nch),
        in_specs=in_specs,
        out_specs=tok(0),
        out_shape=jax.ShapeDtypeStruct((nb, s, D_RET), BF16 if final else F32),
        scratch_shapes=[pltpu.VMEM((RET_HEADS, RET_CHUNK, RET_CHUNK), F32),
                        pltpu.VMEM((RET_CHUNK, D_RET), F32),
                        pltpu.VMEM((RET_CHUNK, D_RET), F32),
                        pltpu.VMEM((1, D_RET), F32),
                        pltpu.VMEM((RET_HEADS // 2, 2 * RET_DK, 2 * RET_DK), F32)],
        compiler_params=_params("parallel", "arbitrary"),
        name="retention_bwd" if rev else "retention_fwd",
    )(*args)


def _merge_kernel(x_ref, mod_ref, g1_ref, wg_ref, bg_ref, a_ref, att_ref, ret_ref,
                  wpa_ref, wpb_ref, wpc_ref, wo_ref, o_ref):
    mod = mod_ref[...]
    x = x_ref[0]
    h = _norm_mod(x, g1_ref[...], mod[0:1], mod[1:2]).astype(BF16)
    merged = None
    for n, (br_ref, w_ref) in enumerate(((a_ref, wpa_ref), (att_ref, wpb_ref), (ret_ref, wpc_ref))):
        sl = slice(n * D_MODEL, (n + 1) * D_MODEL)
        gate = _sigmoid(jnp.dot(h, wg_ref[:, sl], preferred_element_type=F32) + bg_ref[:, sl])
        term = gate * jnp.dot(br_ref[0], w_ref[...], preferred_element_type=F32)
        merged = term if merged is None else merged + term
    y = jnp.dot(merged.astype(BF16), wo_ref[...], preferred_element_type=F32)
    o_ref[0] = x + mod[2:3] * y


def _merge(layer, xa, mods, norm1_g, w_gates, b_gates, a, att, ret, w_pa, w_pb, w_pc, w_out,
           ctx_tiles, skip_ctx):
    nb, s, _ = xa.shape
    off = ctx_tiles if skip_ctx else 0
    nt = s // TILE - off
    wl = lambda shape: pl.BlockSpec((None,) + shape, lambda b, i: (layer,) + (0,) * len(shape))
    tok = lambda w: pl.BlockSpec((1, TILE, w), lambda b, i: (b, i + off, 0))
    return pl.pallas_call(
        _merge_kernel,
        grid=(nb, nt),
        in_specs=[tok(D_MODEL), _mod_spec(layer, nb, ctx_tiles, off), wl((1, D_MODEL)),
                  wl((D_MODEL, N_GATES)), wl((1, N_GATES)),
                  tok(D_CONV), tok(D_Q), tok(D_RET),
                  wl((D_CONV, D_MODEL)), wl((D_Q, D_MODEL)), wl((D_RET, D_MODEL)),
                  wl((D_MODEL, D_MODEL))],
        out_specs=tok(D_MODEL),
        out_shape=jax.ShapeDtypeStruct(xa.shape, F32),
        compiler_params=_params("parallel", "parallel"),
        name="merge",
    )(xa, mods, norm1_g, w_gates, b_gates, a, att, ret, w_pa, w_pb, w_pc, w_out)


def _ffn_kernel(x_ref, xp_ref, xn_ref, mod_ref, g2_ref, wu_ref, dw_ref, db_ref, wd_ref, gf_ref,
                o_ref, gs_ref, *, ctx_tiles, n_tiles, off, final):
    i = pl.program_id(1) + off
    first = (i == 0) | (i == ctx_tiles)
    last = (i == ctx_tiles - 1) | (i == n_tiles - 1)
    mod = mod_ref[...]
    x = x_ref[0]
    xe = jnp.concatenate([xp_ref[0], x, xn_ref[0]], axis=0)
    he = _norm_mod(xe, g2_ref[...], mod[3:4], mod[4:5]).astype(BF16)
    gt = jnp.dot(he, wu_ref[:, :D_FF], preferred_element_type=F32)
    gs_ref[0:FFN_HALO] = jnp.where(first, 0.0, gt[0:FFN_HALO])
    gs_ref[FFN_HALO:FFN_HALO + TILE] = gt[FFN_HALO:FFN_HALO + TILE]
    gs_ref[FFN_HALO + TILE:] = jnp.where(last, 0.0, gt[FFN_HALO + TILE:])
    val = jnp.dot(he[FFN_HALO:FFN_HALO + TILE], wu_ref[:, D_FF:], preferred_element_type=F32)
    conv = db_ref[...]
    for k in range(FFN_K):
        start = FFN_HALO + k - FFN_K // 2
        conv = conv + gs_ref[start:start + TILE, :] * dw_ref[k:k + 1, :]
    act = (_silu(conv) * val).astype(BF16)
    y = x + mod[5:6] * jnp.dot(act, wd_ref[...], preferred_element_type=F32)
    if final:
        y = y * lax.rsqrt(jnp.mean(y * y, axis=-1, keepdims=True) + EPS) * gf_ref[...]
    o_ref[0] = y


def _ffn(layer, xa, mods, norm2_g, w_up, dw_w, dw_b, w_down, final_g, ctx_tiles, final):
    nb, s, _ = xa.shape
    n_tiles = s // TILE
    off = ctx_tiles if final else 0
    nt = n_tiles - off
    hb = TILE // FFN_HALO
    wl = lambda shape: pl.BlockSpec((None,) + shape, lambda b, i: (layer,) + (0,) * len(shape))
    out_rows = s - off * TILE
    return pl.pallas_call(
        functools.partial(_ffn_kernel, ctx_tiles=ctx_tiles, n_tiles=n_tiles, off=off, final=final),
        grid=(nb, nt),
        in_specs=[pl.BlockSpec((1, TILE, D_MODEL), lambda b, i: (b, i + off, 0)),
                  pl.BlockSpec((1, FFN_HALO, D_MODEL),
                               lambda b, i: (b, jnp.maximum((i + off) * hb - 1, 0), 0)),
                  pl.BlockSpec((1, FFN_HALO, D_MODEL),
                               lambda b, i: (b, jnp.minimum((i + off + 1) * hb, n_tiles * hb - 1), 0)),
                  _mod_spec(layer, nb, ctx_tiles, off), wl((1, D_MODEL)),
                  wl((D_MODEL, 2 * D_FF)), wl((FFN_K, D_FF)), wl((1, D_FF)), wl((D_FF, D_MODEL)),
                  pl.BlockSpec((1, D_MODEL), lambda b, i: (0, 0))],
        out_specs=pl.BlockSpec((1, TILE, D_MODEL), lambda b, i: (b, i, 0)),
        out_shape=jax.ShapeDtypeStruct((nb, out_rows, D_MODEL), F32),
        scratch_shapes=[pltpu.VMEM((TILE + 2 * FFN_HALO, D_FF), F32)],
        compiler_params=_params("parallel", "parallel"),
        name="ffn_final" if final else "ffn",
    )(xa, xa, xa, mods, norm2_g, w_up, dw_w, dw_b, w_down, final_g)


def _rope_tables_t(ctx_len, seq_len):
    rows = seq_len // GRID_W
    row = jnp.repeat(jnp.arange(rows, dtype=F32), GRID_W)
    col = jnp.tile(jnp.arange(GRID_W, dtype=F32), rows)
    n_freq = ATT_HD // 4
    inv = ROPE_THETA ** (-jnp.arange(n_freq, dtype=F32) / n_freq)
    ang_r = inv[:, None] * row[None, :]
    ang_c = inv[:, None] * col[None, :]
    tabs = jnp.stack([jnp.cos(ang_r), jnp.sin(ang_r), jnp.cos(ang_c), jnp.sin(ang_c)])
    ident = jnp.stack([jnp.ones((n_freq, ctx_len), F32), jnp.zeros((n_freq, ctx_len), F32)] * 2)
    return jnp.concatenate([ident, tabs], axis=2)


def kernel(x, c, ctx, c_ctx, w_ada, b_ada, norm1_g, w_in, b_in, conv_dw_w, conv_dw_b, conv_ln_g, conv_ln_b, q_norm_g, k_norm_g, ret_decay_logit, ret_gn_g, w_pa, w_pb, w_pc, w_out, norm2_g, w_up, ffn_dw_w, ffn_dw_b, w_down, final_norm_g):
    nb, seq_len, _ = x.shape
    ctx_len = ctx.shape[1]
    assert ctx_len % TILE == 0 and seq_len % TILE == 0 and seq_len % GRID_W == 0
    ctx_tiles = ctx_len // TILE
    ctx_chunks = ctx_len // RET_CHUNK

    xa = jnp.concatenate([ctx, x], axis=1)
    cvec = jnp.zeros((MOD_ROWS, D_MODEL), F32).at[:nb].set(c).at[nb].set(c_ctx)
    rope_t = _rope_tables_t(ctx_len, seq_len)
    row3 = lambda a: a.reshape(DEPTH, 1, a.shape[-1])
    w_main = w_in[:, :, :C_GATES].astype(BF16)
    w_gates = w_in[:, :, C_GATES:].astype(BF16)
    b_main, b_gates = row3(b_in[:, :C_GATES]), row3(b_in[:, C_GATES:])
    qg = jnp.tile(q_norm_g, (1, ATT_HEADS))[:, :, None]
    kg = jnp.tile(k_norm_g, (1, ATT_KV_HEADS))[:, :, None]
    lg_head = jnp.broadcast_to(ret_decay_logit[:, :, :, None, None],
                               (DEPTH, 2, RET_HEADS, 1, 2 * RET_DK)).astype(F32)
    lg_lane = jnp.repeat(ret_decay_logit.astype(F32), RET_DK, axis=-1)[:, :, None, :]
    w_pa_b, w_pb_b, w_pc_b = w_pa.astype(BF16), w_pb.astype(BF16), w_pc.astype(BF16)
    w_out_b, w_up_b, w_down_b = w_out.astype(BF16), w_up.astype(BF16), w_down.astype(BF16)
    n1, n2 = row3(norm1_g), row3(norm2_g)
    cdb, clg, clb = row3(conv_dw_b), row3(conv_ln_g), row3(conv_ln_b)
    fdb, gng = row3(ffn_dw_b), row3(ret_gn_g)
    final_g = final_norm_g.reshape(1, D_MODEL)

    mods = _ada_mod(cvec, w_ada, b_ada).reshape(DEPTH, MOD_ROWS, N_MOD, D_MODEL)

    for layer in range(DEPTH):
        last = layer == DEPTH - 1
        u, qt, k, vt, rqkv, rg = _in_proj(layer, xa, mods, n1, w_main, b_main, qg, kg, rope_t, ctx_tiles)
        a = _conv_module(layer, u, conv_dw_w, cdb, clg, clb, ctx_tiles)
        att = _attention(qt, k, vt, ctx_tiles)
        o_fwd = _retention(layer, rqkv, lg_head, lg_lane, rev=False, ctx_chunks=ctx_chunks)
        ret = _retention(layer, rqkv, lg_head, lg_lane, o_fwd, rg, gng, rev=True, ctx_chunks=ctx_chunks)
        xm = _merge(layer, xa, mods, n1, w_gates, b_gates, a, att, ret,
                    w_pa_b, w_pb_b, w_pc_b, w_out_b, ctx_tiles, skip_ctx=False)
        xa = _ffn(layer, xm, mods, n2, w_up_b, ffn_dw_w, fdb, w_down_b, final_g, ctx_tiles, final=last)
    return xa
```

```python
import functools

import jax
import jax.numpy as jnp
from jax import lax
from jax.experimental import pallas as pl
from jax.experimental.pallas import tpu as pltpu

F32 = jnp.float32
BF16 = jnp.bfloat16

D_MODEL = 1024
DEPTH = 4
GRID_W = 64
N_MOD = 6
EPS = 1e-6
D_CONV = 512
CONV_K = 31
ATT_HEADS = 8
ATT_KV_HEADS = 2
ATT_HD = 64
ATT_GROUPS = ATT_HEADS // ATT_KV_HEADS
ROPE_THETA = 10000.0
RET_HEADS = 8
RET_DK = 64
RET_CHUNK = 128
D_FF = 2816
FFN_K = 3
N_BRANCH = 3

D_Q = ATT_HEADS * ATT_HD
D_KV = ATT_KV_HEADS * ATT_HD
D_RET = RET_HEADS * RET_DK
C_GLU = 0
C_Q = C_GLU + 2 * D_CONV
C_K = C_Q + D_Q
C_V = C_K + D_KV
C_RQ = C_V + D_KV
C_RG = C_RQ + 3 * D_RET
C_GATES = C_RG + D_RET
N_GATES = N_BRANCH * D_MODEL

TILE = 256
HALO = 16
FFN_HALO = 8
V_ROWS = ATT_HD + 16
LOG2E = 1.4426950408889634
MOD_ROWS = 16
V7X_VMEM_LIMIT = 56 * 1024 * 1024


def _params(*sem):
    return pltpu.CompilerParams(dimension_semantics=sem, vmem_limit_bytes=V7X_VMEM_LIMIT)


def _norm_mod(x, g, shift, scale):
    y = x * lax.rsqrt(jnp.mean(x * x, axis=-1, keepdims=True) + EPS)
    return (y * g) * (1.0 + scale) + shift


def _sigmoid(x):
    return 1.0 / (1.0 + jnp.exp(-x))


def _silu(x):
    return x * _sigmoid(x)


def _ada_kernel(c_ref, w_ref, b_ref, o_ref):
    c = c_ref[...]
    o_ref[...] = jnp.dot(_silu(c).astype(BF16), w_ref[...].astype(BF16),
                         preferred_element_type=F32) + b_ref[...]


def _ada_mod(cvec, w_ada, b_ada):
    rows = cvec.shape[0]
    tn = 1536
    n = N_MOD * D_MODEL
    return pl.pallas_call(
        _ada_kernel,
        grid=(DEPTH, n // tn),
        in_specs=[pl.BlockSpec((rows, D_MODEL), lambda l, j: (0, 0)),
                  pl.BlockSpec((None, D_MODEL, tn), lambda l, j: (l, 0, j)),
                  pl.BlockSpec((None, 1, tn), lambda l, j: (l, 0, j))],
        out_specs=pl.BlockSpec((None, rows, tn), lambda l, j: (l, 0, j)),
        out_shape=jax.ShapeDtypeStruct((DEPTH, rows, n), F32),
        compiler_params=_params("parallel", "parallel"),
        name="ada_mod",
    )(cvec, w_ada, b_ada.reshape(DEPTH, 1, n))


def _headnorm_rope_t(xt, g, rope, n_heads):
    cr, sr, cc, sc = rope[0], rope[1], rope[2], rope[3]
    outs = []
    for h in range(n_heads):
        xh = xt[h * ATT_HD:(h + 1) * ATT_HD]
        y = xh * lax.rsqrt(jnp.mean(xh * xh, axis=0, keepdims=True) + EPS)
        y = y * g[h * ATT_HD:(h + 1) * ATT_HD]
        x1r, x2r, x1c, x2c = y[0:16], y[16:32], y[32:48], y[48:64]
        outs += [x1r * cr - x2r * sr, x2r * cr + x1r * sr,
                 x1c * cc - x2c * sc, x2c * cc + x1c * sc]
    return jnp.concatenate(outs, axis=0)


def _in_proj_kernel(x_ref, mod_ref, g1_ref, w_ref, b_ref, qg_ref, kg_ref, rope_ref,
                    u_ref, qt_ref, k_ref, vt_ref, rqkv_ref, rg_ref):
    mod = mod_ref[...]
    h = _norm_mod(x_ref[0], g1_ref[...], mod[0:1], mod[1:2]).astype(BF16)

    def proj(lo, hi):
        return jnp.dot(h, w_ref[:, lo:hi], preferred_element_type=F32) + b_ref[:, lo:hi]

    glu = proj(C_GLU, C_Q)
    u_ref[0] = glu[:, :D_CONV] * _sigmoid(glu[:, D_CONV:])

    rope = rope_ref[...]
    qt = _headnorm_rope_t(proj(C_Q, C_K).T, qg_ref[...], rope, ATT_HEADS) * (ATT_HD ** -0.5 * LOG2E)
    qt_ref[0] = qt.astype(BF16)

    kvt = proj(C_K, C_RQ).T
    kt = _headnorm_rope_t(kvt[:D_KV], kg_ref[...], rope, ATT_KV_HEADS)
    k_ref[0] = kt.T.astype(BF16)
    ones = jnp.ones((V_ROWS - ATT_HD, kvt.shape[1]), BF16)
    for g in range(ATT_KV_HEADS):
        vt_ref[0, 0, g * V_ROWS:g * V_ROWS + ATT_HD] = kvt[D_KV + g * ATT_HD:D_KV + (g + 1) * ATT_HD].astype(BF16)
        vt_ref[0, 0, g * V_ROWS + ATT_HD:(g + 1) * V_ROWS] = ones

    r = proj(C_RQ, C_RG)
    rqkv_ref[0, :, 0:D_RET] = r[:, 0:D_RET].astype(BF16)
    rqkv_ref[0, :, D_RET:2 * D_RET] = (r[:, D_RET:2 * D_RET] * (RET_DK ** -0.5)).astype(BF16)
    rqkv_ref[0, :, 2 * D_RET:] = r[:, 2 * D_RET:].astype(BF16)

    rg_ref[0] = _silu(proj(C_RG, C_GATES))


def _mod_spec(layer, n_batch, ctx_tiles, off=0):
    return pl.BlockSpec((None, None, N_MOD, D_MODEL),
                        lambda b, i: (layer, jnp.where(i + off < ctx_tiles, n_batch, b), 0, 0))


def _in_proj(layer, xa, mods, norm1_g, w_main, b_main, qg, kg, rope_t, ctx_tiles):
    nb, s, _ = xa.shape
    nt = s // TILE
    wl = lambda shape: pl.BlockSpec((None,) + shape, lambda b, i: (layer,) + (0,) * len(shape))
    out_shape = (
        jax.ShapeDtypeStruct((nb, s, D_CONV), F32),
        jax.ShapeDtypeStruct((nb, D_Q, s), BF16),
        jax.ShapeDtypeStruct((nb, s, D_KV), BF16),
        jax.ShapeDtypeStruct((nb, nt, ATT_KV_HEADS * V_ROWS, TILE), BF16),
        jax.ShapeDtypeStruct((nb, s, 3 * D_RET), BF16),
        jax.ShapeDtypeStruct((nb, s, D_RET), F32),
    )
    out_specs = (
        pl.BlockSpec((1, TILE, D_CONV), lambda b, i: (b, i, 0)),
        pl.BlockSpec((1, D_Q, TILE), lambda b, i: (b, 0, i)),
        pl.BlockSpec((1, TILE, D_KV), lambda b, i: (b, i, 0)),
        pl.BlockSpec((1, 1, ATT_KV_HEADS * V_ROWS, TILE), lambda b, i: (b, i, 0, 0)),
        pl.BlockSpec((1, TILE, 3 * D_RET), lambda b, i: (b, i, 0)),
        pl.BlockSpec((1, TILE, D_RET), lambda b, i: (b, i, 0)),
    )
    return pl.pallas_call(
        _in_proj_kernel,
        grid=(nb, nt),
        in_specs=[pl.BlockSpec((1, TILE, D_MODEL), lambda b, i: (b, i, 0)),
                  _mod_spec(layer, nb, ctx_tiles),
                  wl((1, D_MODEL)),
                  wl((D_MODEL, C_GATES)),
                  wl((1, C_GATES)),
                  wl((D_Q, 1)),
                  wl((D_KV, 1)),
                  pl.BlockSpec((4, 16, TILE), lambda b, i: (0, 0, i))],
        out_specs=out_specs,
        out_shape=out_shape,
        compiler_params=_params("parallel", "parallel"),
        name="in_proj",
    )(xa, mods, norm1_g, w_main, b_main, qg, kg, rope_t)


def _conv_kernel(u_ref, up_ref, un_ref, w_ref, b_ref, lg_ref, lb_ref, o_ref, xs_ref,
                 *, ctx_tiles, n_tiles):
    i = pl.program_id(1)
    first = (i == 0) | (i == ctx_tiles)
    last = (i == ctx_tiles - 1) | (i == n_tiles - 1)
    xs_ref[0, 0:HALO] = jnp.where(first, 0.0, up_ref[0])
    xs_ref[0, HALO:HALO + TILE] = u_ref[0]
    xs_ref[0, HALO + TILE:] = jnp.where(last, 0.0, un_ref[0])
    x = xs_ref[0]
    n_rows = x.shape[0]
    for r in range(1, 8):
        xs_ref[r] = pltpu.roll(x, n_rows - r, axis=0)
    rows = 32
    pad = CONV_K // 2
    for c in range(TILE // rows):
        accs = [jnp.zeros((8, D_CONV), F32) for _ in range(rows // 8)]
        for k in range(CONV_K):
            start = HALO + c * rows + k - pad
            r = start % 8
            wk = w_ref[k]
            for j in range(rows // 8):
                lo = start - r + 8 * j
                accs[j] = accs[j] + xs_ref[r, lo:lo + 8, :] * wk
        y = jnp.concatenate(accs, axis=0) + b_ref[...]
        mu = jnp.mean(y, axis=-1, keepdims=True)
        yc = y - mu
        yn = yc * lax.rsqrt(jnp.mean(yc * yc, axis=-1, keepdims=True) + EPS)
        o_ref[0, c * rows:(c + 1) * rows, :] = _silu(yn * lg_ref[...] + lb_ref[...]).astype(BF16)


def _conv_module(layer, u, dw_w, dw_b, ln_g, ln_b, ctx_tiles):
    nb, s, _ = u.shape
    nt = s // TILE
    hb = TILE // HALO
    wl = lambda shape: pl.BlockSpec((None,) + shape, lambda b, i: (layer,) + (0,) * len(shape))
    return pl.pallas_call(
        functools.partial(_conv_kernel, ctx_tiles=ctx_tiles, n_tiles=nt),
        grid=(nb, nt),
        in_specs=[pl.BlockSpec((1, TILE, D_CONV), lambda b, i: (b, i, 0)),
                  pl.BlockSpec((1, HALO, D_CONV), lambda b, i: (b, jnp.maximum(i * hb - 1, 0), 0)),
                  pl.BlockSpec((1, HALO, D_CONV),
                               lambda b, i: (b, jnp.minimum((i + 1) * hb, nt * hb - 1), 0)),
                  wl((CONV_K, 8, D_CONV)), wl((1, D_CONV)), wl((1, D_CONV)), wl((1, D_CONV))],
        out_specs=pl.BlockSpec((1, TILE, D_CONV), lambda b, i: (b, i, 0)),
        out_shape=jax.ShapeDtypeStruct((nb, s, D_CONV), BF16),
        scratch_shapes=[pltpu.VMEM((8, TILE + 2 * HALO, D_CONV), F32)],
        compiler_params=_params("parallel", "parallel"),
        name="conv_module",
    )(u, u, u, dw_w, dw_b, ln_g, ln_b)


def _attn_kernel(qt_ref, k_ref, vt_ref, o_ref, qpad_ref, s_ref, mc_ref, m_ref, acc_ref,
                 *, ctx_tiles, n_tiles):
    g = pl.program_id(1)
    i = pl.program_id(2)
    tq = qt_ref.shape[2]
    mine = (lax.broadcasted_iota(jnp.int32, (2 * ATT_HD, tq), 0) // ATT_HD) == g
    for h in range(ATT_GROUPS):
        qh = qt_ref[0, h * ATT_HD:(h + 1) * ATT_HD, :]
        q2 = jnp.concatenate([qh, qh], axis=0)
        qpad_ref[h] = jnp.where(mine, q2, jnp.zeros_like(q2))
    m_ref[...] = jnp.full(m_ref.shape, -1e30, F32)
    acc_ref[...] = jnp.zeros(acc_ref.shape, F32)

    def scores(j, slot):
        kc = k_ref[0, pl.ds(pl.multiple_of(j * TILE, TILE), TILE), :]
        for h in range(ATT_GROUPS):
            s = jnp.dot(kc, qpad_ref[h], preferred_element_type=F32)
            s_ref[slot, h] = s
            mc_ref[slot, h] = jnp.max(s, axis=0, keepdims=True)

    def update(j, slot):
        vc = vt_ref[0, j]
        for h in range(ATT_GROUPS):
            m_prev = m_ref[h]
            m_new = jnp.maximum(m_prev, mc_ref[slot, h])
            p = jnp.exp2(s_ref[slot, h] - m_new).astype(BF16)
            acc_ref[h] = jnp.exp2(m_prev - m_new) * acc_ref[h] + jnp.dot(
                vc, p, preferred_element_type=F32)
            m_ref[h] = m_new

    def pair(jj, carry):
        j = 2 * jj
        scores(j + 1, 1)
        update(j, 0)
        scores(j + 2, 0)
        update(j + 1, 1)
        return carry

    is_ctx = i < ctx_tiles
    scores(0, 0)
    lax.fori_loop(0, jnp.where(is_ctx, (ctx_tiles - 1) // 2, (n_tiles - 1) // 2), pair, 0)
    update(jnp.where(is_ctx, ctx_tiles - 1, n_tiles - 1), 0)
    o = jnp.concatenate([acc_ref[h, :ATT_HD] / acc_ref[h, ATT_HD:ATT_HD + 1]
                         for h in range(ATT_GROUPS)], axis=0)
    o_ref[0] = o.T.astype(BF16)


def _attention(qt, k, vt, ctx_tiles):
    nb, _, s = qt.shape
    nt = s // TILE
    assert nt % 2 == 1 and ctx_tiles % 2 == 1
    gw = ATT_GROUPS * ATT_HD
    return pl.pallas_call(
        functools.partial(_attn_kernel, ctx_tiles=ctx_tiles, n_tiles=nt),
        grid=(nb, ATT_KV_HEADS, nt),
        in_specs=[pl.BlockSpec((1, gw, TILE), lambda b, g, i: (b, g, i)),
                  pl.BlockSpec((1, s, D_KV), lambda b, g, i: (b, 0, 0)),
                  pl.BlockSpec((1, nt, V_ROWS, TILE), lambda b, g, i: (b, 0, g, 0))],
        out_specs=pl.BlockSpec((1, TILE, gw), lambda b, g, i: (b, i, g)),
        out_shape=jax.ShapeDtypeStruct((nb, s, D_Q), BF16),
        scratch_shapes=[pltpu.VMEM((ATT_GROUPS, 2 * ATT_HD, TILE), BF16),
                        pltpu.VMEM((2, ATT_GROUPS, TILE, TILE), F32),
                        pltpu.VMEM((2, ATT_GROUPS, 1, TILE), F32),
                        pltpu.VMEM((ATT_GROUPS, 1, TILE), F32),
                        pltpu.VMEM((ATT_GROUPS, V_ROWS, TILE), F32)],
        compiler_params=_params("parallel", "parallel", "arbitrary"),
        name="attention",
    )(qt, k, vt)


def _log_sigmoid(x):
    return jnp.minimum(x, 0.0) - jnp.log(1.0 + jnp.exp(-jnp.abs(x)))


def _seg_mean(x, m):
    hi = x.astype(BF16)
    lo = (x - hi.astype(F32)).astype(BF16)
    return (jnp.dot(hi, m, preferred_element_type=F32)
            + jnp.dot(lo, m, preferred_element_type=F32))


def _ret_kernel(*refs, rev, final):
    if final:
        (q_ref, k_ref, v_ref, lgh_ref, lgl_ref, of_ref, rg_ref, gn_ref,
         o_ref, dec_ref, qd_ref, kd_ref, cd_ref, st_ref) = refs
    else:
        (q_ref, k_ref, v_ref, lgh_ref, lgl_ref,
         o_ref, dec_ref, qd_ref, kd_ref, cd_ref, st_ref) = refs
    c = RET_CHUNK
    pw = 2 * RET_DK

    @pl.when(pl.program_id(1) == 0)
    def _():
        ti = lax.broadcasted_iota(jnp.int32, (c, c), 0).astype(F32)
        tj = lax.broadcasted_iota(jnp.int32, (c, c), 1).astype(F32)
        rel = (tj - ti) if rev else (ti - tj)
        keep = rel >= 0
        for h in range(RET_HEADS):
            lg = _log_sigmoid(lgh_ref[h])
            dec_ref[h] = jnp.where(keep, jnp.exp(jnp.where(keep, rel, 0.0) * lg), 0.0)
        lgl = _log_sigmoid(lgl_ref[...])
        t = lax.broadcasted_iota(jnp.int32, (c, D_RET), 0).astype(F32)
        qd_ref[...] = jnp.exp(((c - t) if rev else (t + 1.0)) * lgl)
        kd_ref[...] = jnp.exp((t if rev else (c - 1.0 - t)) * lgl)
        cd_ref[...] = jnp.exp(float(c) * lgl)
        st_ref[...] = jnp.zeros(st_ref.shape, F32)

    lane = lax.broadcasted_iota(jnp.int32, (c, pw), 1)
    row = lax.broadcasted_iota(jnp.int32, (pw, pw), 0)
    col = lax.broadcasted_iota(jnp.int32, (pw, pw), 1)
    same_head = (row >= RET_DK) == (col >= RET_DK)
    avg = jnp.where(same_head, 1.0 / RET_DK, 0.0).astype(BF16)
    nt_dims = (((1,), (1,)), ((), ()))

    head0 = lane < RET_DK
    n_ch = q_ref.shape[1] // c
    order = tuple(range(n_ch - 1, -1, -1) if rev else range(n_ch))
    pairs = tuple(range(RET_HEADS // 2))
    psl = lambda p: slice(p * pw, (p + 1) * pw)
    rsl = lambda ci: slice(ci * c, (ci + 1) * c)

    def split_heads(t, axis):
        z = jnp.zeros_like(t)
        return jnp.concatenate([jnp.where(head0, t, z), jnp.where(head0, z, t)], axis=axis)

    inner, upd = {}, {}
    for ci in order:
        for p in pairs:
            qp, kp, vp = q_ref[0, rsl(ci), psl(p)], k_ref[0, rsl(ci), psl(p)], v_ref[0, rsl(ci), psl(p)]
            inner[ci, p] = lax.dot_general(split_heads(qp, 0), kp, nt_dims,
                                           preferred_element_type=F32)
            kdt = (kp.astype(F32) * kd_ref[:, psl(p)]).T.astype(BF16)
            upd[ci, p] = jnp.dot(kdt, vp, preferred_element_type=F32)

    outs = {}
    for ci in order:
        from_state = {}
        for p in pairs:
            qd = (q_ref[0, rsl(ci), psl(p)].astype(F32) * qd_ref[:, psl(p)]).astype(BF16)
            from_state[p] = jnp.dot(qd, st_ref[p].astype(BF16), preferred_element_type=F32)
        for p in pairs:
            i2 = inner[ci, p]
            a = jnp.concatenate([i2[:c] * dec_ref[2 * p], i2[c:] * dec_ref[2 * p + 1]], axis=1)
            v2 = split_heads(v_ref[0, rsl(ci), psl(p)], 0)
            outs[ci, p] = from_state[p] + jnp.dot(a.astype(BF16), v2, preferred_element_type=F32)
            st_ref[p] = st_ref[p] * cd_ref[:, psl(p)] + jnp.where(same_head, upd[ci, p], 0.0)

    if not final:
        for (ci, p), o in outs.items():
            o_ref[0, rsl(ci), psl(p)] = o
        return
    tot = {key: o + of_ref[0, rsl(key[0]), psl(key[1])] for key, o in outs.items()}
    cen = {key: o - _seg_mean(o, avg) for key, o in tot.items()}
    var = {key: _seg_mean(oc * oc, avg) for key, oc in cen.items()}
    for (ci, p), oc in cen.items():
        y = oc * lax.rsqrt(var[ci, p] + EPS) * gn_ref[:, psl(p)]
        o_ref[0, rsl(ci), psl(p)] = (y * rg_ref[0, rsl(ci), psl(p)]).astype(o_ref.dtype)


def _retention(layer, rqkv, lg_head, lg_lane, o_fwd=None, rg=None, gn_g=None, *, rev, ctx_tiles):
    nb, s, _ = rqkv.shape
    nt = s // TILE
    final = o_fwd is not None
    d = 1 if rev else 0
    if rev:
        cmap = lambda j: jnp.where(j < ctx_tiles, ctx_tiles - 1 - j, nt - 1 - (j - ctx_tiles))
    else:
        cmap = lambda j: j
    tok = lambda col: pl.BlockSpec((1, TILE, D_RET), lambda b, j: (b, cmap(j), col))
    in_specs = [tok(0), tok(1), tok(2),
                pl.BlockSpec((None, None, RET_HEADS, 1, 2 * RET_DK), lambda b, j: (layer, d, 0, 0, 0)),
                pl.BlockSpec((None, None, 1, D_RET), lambda b, j: (layer, d, 0, 0))]
    args = [rqkv, rqkv, rqkv, lg_head, lg_lane]
    if final:
        in_specs += [tok(0), tok(0), pl.BlockSpec((None, 1, D_RET), lambda b, j: (layer, 0, 0))]
        args += [o_fwd, rg, gn_g]
    return pl.pallas_call(
        functools.partial(_ret_kernel, rev=rev, final=final),
        grid=(nb, nt),
        in_specs=in_specs,
        out_specs=tok(0),
        out_shape=jax.ShapeDtypeStruct((nb, s, D_RET), BF16 if final else F32),
        scratch_shapes=[pltpu.VMEM((RET_HEADS, RET_CHUNK, RET_CHUNK), F32),
                        pltpu.VMEM((RET_CHUNK, D_RET), F32),
                        pltpu.VMEM((RET_CHUNK, D_RET), F32),
                        pltpu.VMEM((1, D_RET), F32),
                        pltpu.VMEM((RET_HEADS // 2, 2 * RET_DK, 2 * RET_DK), F32)],
        compiler_params=_params("parallel", "arbitrary"),
        name="retention_bwd" if rev else "retention_fwd",
    )(*args)


def _merge_kernel(x_ref, mod_ref, g1_ref, wg_ref, bg_ref, a_ref, att_ref, ret_ref,
                  wpa_ref, wpb_ref, wpc_ref, wo_ref, o_ref):
    mod = mod_ref[...]
    x = x_ref[0]
    h = _norm_mod(x, g1_ref[...], mod[0:1], mod[1:2]).astype(BF16)
    merged = None
    for n, (br_ref, w_ref) in enumerate(((a_ref, wpa_ref), (att_ref, wpb_ref), (ret_ref, wpc_ref))):
        sl = slice(n * D_MODEL, (n + 1) * D_MODEL)
        gate = _sigmoid(jnp.dot(h, wg_ref[:, sl], preferred_element_type=F32) + bg_ref[:, sl])
        term = gate * jnp.dot(br_ref[0], w_ref[...], preferred_element_type=F32)
        merged = term if merged is None else merged + term
    y = jnp.dot(merged.astype(BF16), wo_ref[...], preferred_element_type=F32)
    o_ref[0] = x + mod[2:3] * y


def _merge(layer, xa, mods, norm1_g, w_gates, b_gates, a, att, ret, w_pa, w_pb, w_pc, w_out,
           ctx_tiles, skip_ctx):
    nb, s, _ = xa.shape
    off = ctx_tiles if skip_ctx else 0
    nt = s // TILE - off
    wl = lambda shape: pl.BlockSpec((None,) + shape, lambda b, i: (layer,) + (0,) * len(shape))
    tok = lambda w: pl.BlockSpec((1, TILE, w), lambda b, i: (b, i + off, 0))
    return pl.pallas_call(
        _merge_kernel,
        grid=(nb, nt),
        in_specs=[tok(D_MODEL), _mod_spec(layer, nb, ctx_tiles, off), wl((1, D_MODEL)),
                  wl((D_MODEL, N_GATES)), wl((1, N_GATES)),
                  tok(D_CONV), tok(D_Q), tok(D_RET),
                  wl((D_CONV, D_MODEL)), wl((D_Q, D_MODEL)), wl((D_RET, D_MODEL)),
                  wl((D_MODEL, D_MODEL))],
        out_specs=tok(D_MODEL),
        out_shape=jax.ShapeDtypeStruct(xa.shape, F32),
        compiler_params=_params("parallel", "parallel"),
        name="merge",
    )(xa, mods, norm1_g, w_gates, b_gates, a, att, ret, w_pa, w_pb, w_pc, w_out)


def _ffn_kernel(x_ref, xp_ref, xn_ref, mod_ref, g2_ref, wu_ref, dw_ref, db_ref, wd_ref, gf_ref,
                o_ref, gs_ref, *, ctx_tiles, n_tiles, off, final):
    i = pl.program_id(1) + off
    first = (i == 0) | (i == ctx_tiles)
    last = (i == ctx_tiles - 1) | (i == n_tiles - 1)
    mod = mod_ref[...]
    x = x_ref[0]
    xe = jnp.concatenate([xp_ref[0], x, xn_ref[0]], axis=0)
    he = _norm_mod(xe, g2_ref[...], mod[3:4], mod[4:5]).astype(BF16)
    gt = jnp.dot(he, wu_ref[:, :D_FF], preferred_element_type=F32)
    gs_ref[0:FFN_HALO] = jnp.where(first, 0.0, gt[0:FFN_HALO])
    gs_ref[FFN_HALO:FFN_HALO + TILE] = gt[FFN_HALO:FFN_HALO + TILE]
    gs_ref[FFN_HALO + TILE:] = jnp.where(last, 0.0, gt[FFN_HALO + TILE:])
    val = jnp.dot(he[FFN_HALO:FFN_HALO + TILE], wu_ref[:, D_FF:], preferred_element_type=F32)
    conv = db_ref[...]
    for k in range(FFN_K):
        start = FFN_HALO + k - FFN_K // 2
        conv = conv + gs_ref[start:start + TILE, :] * dw_ref[k:k + 1, :]
    act = (_silu(conv) * val).astype(BF16)
    y = x + mod[5:6] * jnp.dot(act, wd_ref[...], preferred_element_type=F32)
    if final:
        y = y * lax.rsqrt(jnp.mean(y * y, axis=-1, keepdims=True) + EPS) * gf_ref[...]
    o_ref[0] = y


def _ffn(layer, xa, mods, norm2_g, w_up, dw_w, dw_b, w_down, final_g, ctx_tiles, final):
    nb, s, _ = xa.shape
    n_tiles = s // TILE
    off = ctx_tiles if final else 0
    nt = n_tiles - off
    hb = TILE // FFN_HALO
    wl = lambda shape: pl.BlockSpec((None,) + shape, lambda b, i: (layer,) + (0,) * len(shape))
    out_rows = s - off * TILE
    return pl.pallas_call(
        functools.partial(_ffn_kernel, ctx_tiles=ctx_tiles, n_tiles=n_tiles, off=off, final=final),
        grid=(nb, nt),
        in_specs=[pl.BlockSpec((1, TILE, D_MODEL), lambda b, i: (b, i + off, 0)),
                  pl.BlockSpec((1, FFN_HALO, D_MODEL),
                               lambda b, i: (b, jnp.maximum((i + off) * hb - 1, 0), 0)),
                  pl.BlockSpec((1, FFN_HALO, D_MODEL),
                               lambda b, i: (b, jnp.minimum((i + off + 1) * hb, n_tiles * hb - 1), 0)),
                  _mod_spec(layer, nb, ctx_tiles, off), wl((1, D_MODEL)),
                  wl((D_MODEL, 2 * D_FF)), wl((FFN_K, D_FF)), wl((1, D_FF)), wl((D_FF, D_MODEL)),
                  pl.BlockSpec((1, D_MODEL), lambda b, i: (0, 0))],
        out_specs=pl.BlockSpec((1, TILE, D_MODEL), lambda b, i: (b, i, 0)),
        out_shape=jax.ShapeDtypeStruct((nb, out_rows, D_MODEL), F32),
        scratch_shapes=[pltpu.VMEM((TILE + 2 * FFN_HALO, D_FF), F32)],
        compiler_params=_params("parallel", "parallel"),
        name="ffn_final" if final else "ffn",
    )(xa, xa, xa, mods, norm2_g, w_up, dw_w, dw_b, w_down, final_g)


def _rope_tables_t(ctx_len, seq_len):
    rows = seq_len // GRID_W
    row = jnp.repeat(jnp.arange(rows, dtype=F32), GRID_W)
    col = jnp.tile(jnp.arange(GRID_W, dtype=F32), rows)
    n_freq = ATT_HD // 4
    inv = ROPE_THETA ** (-jnp.arange(n_freq, dtype=F32) / n_freq)
    ang_r = inv[:, None] * row[None, :]
    ang_c = inv[:, None] * col[None, :]
    tabs = jnp.stack([jnp.cos(ang_r), jnp.sin(ang_r), jnp.cos(ang_c), jnp.sin(ang_c)])
    ident = jnp.stack([jnp.ones((n_freq, ctx_len), F32), jnp.zeros((n_freq, ctx_len), F32)] * 2)
    return jnp.concatenate([ident, tabs], axis=2)


def kernel(x, c, ctx, c_ctx, w_ada, b_ada, norm1_g, w_in, b_in, conv_dw_w, conv_dw_b, conv_ln_g, conv_ln_b, q_norm_g, k_norm_g, ret_decay_logit, ret_gn_g, w_pa, w_pb, w_pc, w_out, norm2_g, w_up, ffn_dw_w, ffn_dw_b, w_down, final_norm_g):
    nb, seq_len, _ = x.shape
    ctx_len = ctx.shape[1]
    assert ctx_len % TILE == 0 and seq_len % TILE == 0 and seq_len % GRID_W == 0
    ctx_tiles = ctx_len // TILE

    xa = jnp.concatenate([ctx, x], axis=1)
    cvec = jnp.zeros((MOD_ROWS, D_MODEL), F32).at[:nb].set(c).at[nb].set(c_ctx)
    rope_t = _rope_tables_t(ctx_len, seq_len)
    row3 = lambda a: a.reshape(DEPTH, 1, a.shape[-1])
    w_main = w_in[:, :, :C_GATES].astype(BF16)
    w_gates = w_in[:, :, C_GATES:].astype(BF16)
    b_main, b_gates = row3(b_in[:, :C_GATES]), row3(b_in[:, C_GATES:])
    qg = jnp.tile(q_norm_g, (1, ATT_HEADS))[:, :, None]
    kg = jnp.tile(k_norm_g, (1, ATT_KV_HEADS))[:, :, None]
    lg_head = jnp.broadcast_to(ret_decay_logit[:, :, :, None, None],
                               (DEPTH, 2, RET_HEADS, 1, 2 * RET_DK)).astype(F32)
    lg_lane = jnp.repeat(ret_decay_logit.astype(F32), RET_DK, axis=-1)[:, :, None, :]
    w_pa_b, w_pb_b, w_pc_b = w_pa.astype(BF16), w_pb.astype(BF16), w_pc.astype(BF16)
    w_out_b, w_up_b, w_down_b = w_out.astype(BF16), w_up.astype(BF16), w_down.astype(BF16)
    n1, n2 = row3(norm1_g), row3(norm2_g)
    cdb, clg, clb = row3(conv_dw_b), row3(conv_ln_g), row3(conv_ln_b)
    cdw = jnp.broadcast_to(conv_dw_w[:, :, None, :], (DEPTH, CONV_K, 8, D_CONV))
    fdb, gng = row3(ffn_dw_b), row3(ret_gn_g)
    final_g = final_norm_g.reshape(1, D_MODEL)

    mods = _ada_mod(cvec, w_ada, b_ada).reshape(DEPTH, MOD_ROWS, N_MOD, D_MODEL)

    for layer in range(DEPTH):
        last = layer == DEPTH - 1
        u, qt, k, vt, rqkv, rg = _in_proj(layer, xa, mods, n1, w_main, b_main, qg, kg, rope_t, ctx_tiles)
        a = _conv_module(layer, u, cdw, cdb, clg, clb, ctx_tiles)
        att = _attention(qt, k, vt, ctx_tiles)
        o_fwd = _retention(layer, rqkv, lg_head, lg_lane, rev=False, ctx_tiles=ctx_tiles)
        ret = _retention(layer, rqkv, lg_head, lg_lane, o_fwd, rg, gng, rev=True, ctx_tiles=ctx_tiles)
        xm = _merge(layer, xa, mods, n1, w_gates, b_gates, a, att, ret,
                    w_pa_b, w_pb_b, w_pc_b, w_out_b, ctx_tiles, skip_ctx=False)
        xa = _ffn(layer, xm, mods, n2, w_up_b, ffn_dw_w, fdb, w_down_b, final_g, ctx_tiles, final=last)
    return xa
```

```python
import functools

import jax
import jax.numpy as jnp
from jax import lax
from jax.experimental import pallas as pl
from jax.experimental.pallas import tpu as pltpu

F32 = jnp.float32
BF16 = jnp.bfloat16

D_MODEL = 1024
DEPTH = 4
GRID_W = 64
N_MOD = 6
EPS = 1e-6
D_CONV = 512
CONV_K = 31
ATT_HEADS = 8
ATT_KV_HEADS = 2
ATT_HD = 64
ATT_GROUPS = ATT_HEADS // ATT_KV_HEADS
ROPE_THETA = 10000.0
RET_HEADS = 8
RET_DK = 64
RET_CHUNK = 128
D_FF = 2816
FFN_K = 3
N_BRANCH = 3

D_Q = ATT_HEADS * ATT_HD
D_KV = ATT_KV_HEADS * ATT_HD
D_RET = RET_HEADS * RET_DK
C_GLU = 0
C_Q = C_GLU + 2 * D_CONV
C_K = C_Q + D_Q
C_V = C_K + D_KV
C_RQ = C_V + D_KV
C_RG = C_RQ + 3 * D_RET
C_GATES = C_RG + D_RET
N_GATES = N_BRANCH * D_MODEL

TILE = 256
HALO = 16
FFN_HALO = 8
V_ROWS = ATT_HD + 16
LOG2E = 1.4426950408889634
MOD_ROWS = 16
V7X_VMEM_LIMIT = 56 * 1024 * 1024


def _params(*sem):
    return pltpu.CompilerParams(dimension_semantics=sem, vmem_limit_bytes=V7X_VMEM_LIMIT)


def _norm_mod(x, g, shift, scale):
    y = x * lax.rsqrt(jnp.mean(x * x, axis=-1, keepdims=True) + EPS)
    return (y * g) * (1.0 + scale) + shift


def _sigmoid(x):
    return 1.0 / (1.0 + jnp.exp(-x))


def _silu(x):
    return x * _sigmoid(x)


def _ada_kernel(c_ref, w_ref, b_ref, o_ref):
    c = c_ref[...]
    o_ref[...] = jnp.dot(_silu(c).astype(BF16), w_ref[...].astype(BF16),
                         preferred_element_type=F32) + b_ref[...]


def _ada_mod(cvec, w_ada, b_ada):
    rows = cvec.shape[0]
    tn = 1536
    n = N_MOD * D_MODEL
    return pl.pallas_call(
        _ada_kernel,
        grid=(DEPTH, n // tn),
        in_specs=[pl.BlockSpec((rows, D_MODEL), lambda l, j: (0, 0)),
                  pl.BlockSpec((None, D_MODEL, tn), lambda l, j: (l, 0, j)),
                  pl.BlockSpec((None, 1, tn), lambda l, j: (l, 0, j))],
        out_specs=pl.BlockSpec((None, rows, tn), lambda l, j: (l, 0, j)),
        out_shape=jax.ShapeDtypeStruct((DEPTH, rows, n), F32),
        compiler_params=_params("parallel", "parallel"),
        name="ada_mod",
    )(cvec, w_ada, b_ada.reshape(DEPTH, 1, n))


def _headnorm_rope_t(xt, g, rope, n_heads):
    cr, sr, cc, sc = rope[0], rope[1], rope[2], rope[3]
    outs = []
    for h in range(n_heads):
        xh = xt[h * ATT_HD:(h + 1) * ATT_HD]
        y = xh * lax.rsqrt(jnp.mean(xh * xh, axis=0, keepdims=True) + EPS)
        y = y * g[h * ATT_HD:(h + 1) * ATT_HD]
        x1r, x2r, x1c, x2c = y[0:16], y[16:32], y[32:48], y[48:64]
        outs += [x1r * cr - x2r * sr, x2r * cr + x1r * sr,
                 x1c * cc - x2c * sc, x2c * cc + x1c * sc]
    return jnp.concatenate(outs, axis=0)


def _in_proj_kernel(x_ref, mod_ref, g1_ref, w_ref, b_ref, qg_ref, kg_ref, rope_ref,
                    u_ref, qt_ref, k_ref, vt_ref, rqkv_ref, rg_ref):
    mod = mod_ref[...]
    h = _norm_mod(x_ref[0], g1_ref[...], mod[0:1], mod[1:2]).astype(BF16)

    def proj(lo, hi):
        return jnp.dot(h, w_ref[:, lo:hi], preferred_element_type=F32) + b_ref[:, lo:hi]

    glu = proj(C_GLU, C_Q)
    u_ref[0] = glu[:, :D_CONV] * _sigmoid(glu[:, D_CONV:])

    rope = rope_ref[...]
    qt = _headnorm_rope_t(proj(C_Q, C_K).T, qg_ref[...], rope, ATT_HEADS) * (ATT_HD ** -0.5 * LOG2E)
    qt_ref[0] = qt.astype(BF16)

    kvt = proj(C_K, C_RQ).T
    kt = _headnorm_rope_t(kvt[:D_KV], kg_ref[...], rope, ATT_KV_HEADS)
    k_ref[0] = kt.T.astype(BF16)
    ones = jnp.ones((V_ROWS - ATT_HD, kvt.shape[1]), BF16)
    for g in range(ATT_KV_HEADS):
        vt_ref[0, 0, g * V_ROWS:g * V_ROWS + ATT_HD] = kvt[D_KV + g * ATT_HD:D_KV + (g + 1) * ATT_HD].astype(BF16)
        vt_ref[0, 0, g * V_ROWS + ATT_HD:(g + 1) * V_ROWS] = ones

    r = proj(C_RQ, C_RG)
    rqkv_ref[0, :, 0:D_RET] = r[:, 0:D_RET].astype(BF16)
    rqkv_ref[0, :, D_RET:2 * D_RET] = (r[:, D_RET:2 * D_RET] * (RET_DK ** -0.5)).astype(BF16)
    rqkv_ref[0, :, 2 * D_RET:] = r[:, 2 * D_RET:].astype(BF16)

    rg_ref[0] = _silu(proj(C_RG, C_GATES))


def _mod_spec(layer, n_batch, ctx_tiles, off=0):
    return pl.BlockSpec((None, None, N_MOD, D_MODEL),
                        lambda b, i: (layer, jnp.where(i + off < ctx_tiles, n_batch, b), 0, 0))


def _in_proj(layer, xa, mods, norm1_g, w_main, b_main, qg, kg, rope_t, ctx_tiles):
    nb, s, _ = xa.shape
    nt = s // TILE
    wl = lambda shape: pl.BlockSpec((None,) + shape, lambda b, i: (layer,) + (0,) * len(shape))
    out_shape = (
        jax.ShapeDtypeStruct((nb, s, D_CONV), F32),
        jax.ShapeDtypeStruct((nb, D_Q, s), BF16),
        jax.ShapeDtypeStruct((nb, s, D_KV), BF16),
        jax.ShapeDtypeStruct((nb, nt, ATT_KV_HEADS * V_ROWS, TILE), BF16),
        jax.ShapeDtypeStruct((nb, s, 3 * D_RET), BF16),
        jax.ShapeDtypeStruct((nb, s, D_RET), F32),
    )
    out_specs = (
        pl.BlockSpec((1, TILE, D_CONV), lambda b, i: (b, i, 0)),
        pl.BlockSpec((1, D_Q, TILE), lambda b, i: (b, 0, i)),
        pl.BlockSpec((1, TILE, D_KV), lambda b, i: (b, i, 0)),
        pl.BlockSpec((1, 1, ATT_KV_HEADS * V_ROWS, TILE), lambda b, i: (b, i, 0, 0)),
        pl.BlockSpec((1, TILE, 3 * D_RET), lambda b, i: (b, i, 0)),
        pl.BlockSpec((1, TILE, D_RET), lambda b, i: (b, i, 0)),
    )
    return pl.pallas_call(
        _in_proj_kernel,
        grid=(nb, nt),
        in_specs=[pl.BlockSpec((1, TILE, D_MODEL), lambda b, i: (b, i, 0)),
                  _mod_spec(layer, nb, ctx_tiles),
                  wl((1, D_MODEL)),
                  wl((D_MODEL, C_GATES)),
                  wl((1, C_GATES)),
                  wl((D_Q, 1)),
                  wl((D_KV, 1)),
                  pl.BlockSpec((4, 16, TILE), lambda b, i: (0, 0, i))],
        out_specs=out_specs,
        out_shape=out_shape,
        compiler_params=_params("parallel", "parallel"),
        name="in_proj",
    )(xa, mods, norm1_g, w_main, b_main, qg, kg, rope_t)


def _conv_kernel(u_ref, up_ref, un_ref, w_ref, b_ref, lg_ref, lb_ref, o_ref, xs_ref,
                 *, ctx_tiles, n_tiles):
    i = pl.program_id(1)
    first = (i == 0) | (i == ctx_tiles)
    last = (i == ctx_tiles - 1) | (i == n_tiles - 1)
    xs_ref[0, 0:HALO] = jnp.where(first, 0.0, up_ref[0])
    xs_ref[0, HALO:HALO + TILE] = u_ref[0]
    xs_ref[0, HALO + TILE:] = jnp.where(last, 0.0, un_ref[0])
    x = xs_ref[0]
    n_rows = x.shape[0]
    for r in range(1, 8):
        xs_ref[r] = pltpu.roll(x, n_rows - r, axis=0)
    rows = 32
    pad = CONV_K // 2
    for c in range(TILE // rows):
        accs = [jnp.zeros((8, D_CONV), F32) for _ in range(rows // 8)]
        for k in range(CONV_K):
            start = HALO + c * rows + k - pad
            r = start % 8
            wk = w_ref[k]
            for j in range(rows // 8):
                lo = start - r + 8 * j
                accs[j] = accs[j] + xs_ref[r, lo:lo + 8, :] * wk
        y = jnp.concatenate(accs, axis=0) + b_ref[...]
        mu = jnp.mean(y, axis=-1, keepdims=True)
        yc = y - mu
        yn = yc * lax.rsqrt(jnp.mean(yc * yc, axis=-1, keepdims=True) + EPS)
        o_ref[0, c * rows:(c + 1) * rows, :] = _silu(yn * lg_ref[...] + lb_ref[...]).astype(BF16)


def _conv_module(layer, u, dw_w, dw_b, ln_g, ln_b, ctx_tiles):
    nb, s, _ = u.shape
    nt = s // TILE
    hb = TILE // HALO
    wl = lambda shape: pl.BlockSpec((None,) + shape, lambda b, i: (layer,) + (0,) * len(shape))
    return pl.pallas_call(
        functools.partial(_conv_kernel, ctx_tiles=ctx_tiles, n_tiles=nt),
        grid=(nb, nt),
        in_specs=[pl.BlockSpec((1, TILE, D_CONV), lambda b, i: (b, i, 0)),
                  pl.BlockSpec((1, HALO, D_CONV), lambda b, i: (b, jnp.maximum(i * hb - 1, 0), 0)),
                  pl.BlockSpec((1, HALO, D_CONV),
                               lambda b, i: (b, jnp.minimum((i + 1) * hb, nt * hb - 1), 0)),
                  wl((CONV_K, 8, D_CONV)), wl((1, D_CONV)), wl((1, D_CONV)), wl((1, D_CONV))],
        out_specs=pl.BlockSpec((1, TILE, D_CONV), lambda b, i: (b, i, 0)),
        out_shape=jax.ShapeDtypeStruct((nb, s, D_CONV), BF16),
        scratch_shapes=[pltpu.VMEM((8, TILE + 2 * HALO, D_CONV), F32)],
        compiler_params=_params("parallel", "parallel"),
        name="conv_module",
    )(u, u, u, dw_w, dw_b, ln_g, ln_b)


def _attn_kernel(qt_ref, k_ref, vt_ref, o_ref, qpad_ref, s_ref, mc_ref, m_ref, acc_ref,
                 *, ctx_tiles, n_tiles, unroll, variant):
    g = pl.program_id(1)
    i = pl.program_id(2)
    tq = qt_ref.shape[2]
    mine = (lax.broadcasted_iota(jnp.int32, (2 * ATT_HD, tq), 0) // ATT_HD) == g
    for h in range(ATT_GROUPS):
        qh = qt_ref[0, h * ATT_HD:(h + 1) * ATT_HD, :]
        q2 = jnp.concatenate([qh, qh], axis=0)
        qpad_ref[h] = jnp.where(mine, q2, jnp.zeros_like(q2))
    m_ref[...] = jnp.full(m_ref.shape, -1e30, F32)
    acc_ref[...] = jnp.zeros(acc_ref.shape, F32)

    def scores(j, slot):
        kc = k_ref[0, pl.ds(pl.multiple_of(j * TILE, TILE), TILE), :]
        for h in range(ATT_GROUPS):
            s = jnp.dot(kc, qpad_ref[h], preferred_element_type=F32)
            s_ref[slot, h] = s
            mc_ref[slot, h] = jnp.max(s, axis=0, keepdims=True)

    def update(j, slot):
        vc = vt_ref[0, j]
        for h in range(ATT_GROUPS):
            m_prev = m_ref[h]
            m_new = jnp.maximum(m_prev, mc_ref[slot, h])
            alpha = jnp.exp2(m_prev - m_new)
            if variant == "vpusum":
                pf = jnp.exp2(s_ref[slot, h] - m_new)
                acc_ref[h, :ATT_HD] = alpha * acc_ref[h, :ATT_HD] + jnp.dot(
                    vc[:ATT_HD], pf.astype(BF16), preferred_element_type=F32)
                acc_ref[h, ATT_HD:ATT_HD + 1] = (alpha * acc_ref[h, ATT_HD:ATT_HD + 1]
                                                 + jnp.sum(pf, axis=0, keepdims=True))
            else:
                if variant == "bf16exp":
                    p = jnp.exp2((s_ref[slot, h] - m_new).astype(BF16))
                else:
                    p = jnp.exp2(s_ref[slot, h] - m_new).astype(BF16)
                acc_ref[h] = alpha * acc_ref[h] + jnp.dot(vc, p, preferred_element_type=F32)
            m_ref[h] = m_new

    def group(jj, carry):
        j = unroll * jj
        for t in range(unroll):
            scores(j + t + 1, (t + 1) % 2)
            update(j + t, t % 2)
        return carry

    is_ctx = i < ctx_tiles
    scores(0, 0)
    lax.fori_loop(0, jnp.where(is_ctx, (ctx_tiles - 1) // unroll, (n_tiles - 1) // unroll), group, 0)
    update(jnp.where(is_ctx, ctx_tiles - 1, n_tiles - 1), 0)
    o = jnp.concatenate([acc_ref[h, :ATT_HD] / acc_ref[h, ATT_HD:ATT_HD + 1]
                         for h in range(ATT_GROUPS)], axis=0)
    o_ref[0] = o.T.astype(BF16)


def _attention(qt, k, vt, ctx_tiles, variant):
    nb, _, s = qt.shape
    nt = s // TILE
    sizes = (8, 4, 2) if variant == "u8" else (16, 8, 4, 2)
    unroll = next(u for u in sizes if (nt - 1) % u == 0 and (ctx_tiles - 1) % u == 0)
    gw = ATT_GROUPS * ATT_HD
    return pl.pallas_call(
        functools.partial(_attn_kernel, ctx_tiles=ctx_tiles, n_tiles=nt, unroll=unroll, variant=variant),
        grid=(nb, ATT_KV_HEADS, nt),
        in_specs=[pl.BlockSpec((1, gw, TILE), lambda b, g, i: (b, g, i)),
                  pl.BlockSpec((1, s, D_KV), lambda b, g, i: (b, 0, 0)),
                  pl.BlockSpec((1, nt, V_ROWS, TILE), lambda b, g, i: (b, 0, g, 0))],
        out_specs=pl.BlockSpec((1, TILE, gw), lambda b, g, i: (b, i, g)),
        out_shape=jax.ShapeDtypeStruct((nb, s, D_Q), BF16),
        scratch_shapes=[pltpu.VMEM((ATT_GROUPS, 2 * ATT_HD, TILE), BF16),
                        pltpu.VMEM((2, ATT_GROUPS, TILE, TILE), F32),
                        pltpu.VMEM((2, ATT_GROUPS, 1, TILE), F32),
                        pltpu.VMEM((ATT_GROUPS, 1, TILE), F32),
                        pltpu.VMEM((ATT_GROUPS, V_ROWS, TILE), F32)],
        compiler_params=_params("parallel", "parallel", "arbitrary"),
        name="attention",
    )(qt, k, vt)


def _log_sigmoid(x):
    return jnp.minimum(x, 0.0) - jnp.log(1.0 + jnp.exp(-jnp.abs(x)))


def _seg_mean(x, m):
    hi = x.astype(BF16)
    lo = (x - hi.astype(F32)).astype(BF16)
    return (jnp.dot(hi, m, preferred_element_type=F32)
            + jnp.dot(lo, m, preferred_element_type=F32))


def _ret_kernel(*refs, rev, final):
    if final:
        (q_ref, k_ref, v_ref, lgh_ref, lgl_ref, of_ref, rg_ref, gn_ref,
         o_ref, dec_ref, qd_ref, kd_ref, cd_ref, st_ref) = refs
    else:
        (q_ref, k_ref, v_ref, lgh_ref, lgl_ref,
         o_ref, dec_ref, qd_ref, kd_ref, cd_ref, st_ref) = refs
    c = RET_CHUNK
    pw = 2 * RET_DK

    @pl.when(pl.program_id(1) == 0)
    def _():
        ti = lax.broadcasted_iota(jnp.int32, (c, c), 0).astype(F32)
        tj = lax.broadcasted_iota(jnp.int32, (c, c), 1).astype(F32)
        rel = (tj - ti) if rev else (ti - tj)
        keep = rel >= 0
        for h in range(RET_HEADS):
            lg = _log_sigmoid(lgh_ref[h])
            dec_ref[h] = jnp.where(keep, jnp.exp(jnp.where(keep, rel, 0.0) * lg), 0.0)
        lgl = _log_sigmoid(lgl_ref[...])
        t = lax.broadcasted_iota(jnp.int32, (c, D_RET), 0).astype(F32)
        qd_ref[...] = jnp.exp(((c - t) if rev else (t + 1.0)) * lgl)
        kd_ref[...] = jnp.exp((t if rev else (c - 1.0 - t)) * lgl)
        cd_ref[...] = jnp.exp(float(c) * lgl)
        st_ref[...] = jnp.zeros(st_ref.shape, F32)

    lane = lax.broadcasted_iota(jnp.int32, (c, pw), 1)
    row = lax.broadcasted_iota(jnp.int32, (pw, pw), 0)
    col = lax.broadcasted_iota(jnp.int32, (pw, pw), 1)
    same_head = (row >= RET_DK) == (col >= RET_DK)
    avg = jnp.where(same_head, 1.0 / RET_DK, 0.0).astype(BF16)
    nt_dims = (((1,), (1,)), ((), ()))

    head0 = lane < RET_DK
    n_ch = q_ref.shape[1] // c
    order = tuple(range(n_ch - 1, -1, -1) if rev else range(n_ch))
    pairs = tuple(range(RET_HEADS // 2))
    psl = lambda p: slice(p * pw, (p + 1) * pw)
    rsl = lambda ci: slice(ci * c, (ci + 1) * c)

    def split_heads(t, axis):
        z = jnp.zeros_like(t)
        return jnp.concatenate([jnp.where(head0, t, z), jnp.where(head0, z, t)], axis=axis)

    inner, upd = {}, {}
    for ci in order:
        for p in pairs:
            qp, kp, vp = q_ref[0, rsl(ci), psl(p)], k_ref[0, rsl(ci), psl(p)], v_ref[0, rsl(ci), psl(p)]
            inner[ci, p] = lax.dot_general(split_heads(qp, 0), kp, nt_dims,
                                           preferred_element_type=F32)
            kdt = (kp.astype(F32) * kd_ref[:, psl(p)]).T.astype(BF16)
            upd[ci, p] = jnp.dot(kdt, vp, preferred_element_type=F32)

    outs = {}
    for ci in order:
        from_state = {}
        for p in pairs:
            qd = (q_ref[0, rsl(ci), psl(p)].astype(F32) * qd_ref[:, psl(p)]).astype(BF16)
            from_state[p] = jnp.dot(qd, st_ref[p].astype(BF16), preferred_element_type=F32)
        for p in pairs:
            i2 = inner[ci, p]
            a = jnp.concatenate([i2[:c] * dec_ref[2 * p], i2[c:] * dec_ref[2 * p + 1]], axis=1)
            v2 = split_heads(v_ref[0, rsl(ci), psl(p)], 0)
            outs[ci, p] = from_state[p] + jnp.dot(a.astype(BF16), v2, preferred_element_type=F32)
            st_ref[p] = st_ref[p] * cd_ref[:, psl(p)] + jnp.where(same_head, upd[ci, p], 0.0)

    if not final:
        for (ci, p), o in outs.items():
            o_ref[0, rsl(ci), psl(p)] = o
        return
    tot = {key: o + of_ref[0, rsl(key[0]), psl(key[1])] for key, o in outs.items()}
    cen = {key: o - _seg_mean(o, avg) for key, o in tot.items()}
    var = {key: _seg_mean(oc * oc, avg) for key, oc in cen.items()}
    for (ci, p), oc in cen.items():
        y = oc * lax.rsqrt(var[ci, p] + EPS) * gn_ref[:, psl(p)]
        o_ref[0, rsl(ci), psl(p)] = (y * rg_ref[0, rsl(ci), psl(p)]).astype(o_ref.dtype)


def _retention(layer, rqkv, lg_head, lg_lane, o_fwd=None, rg=None, gn_g=None, *, rev, ctx_tiles):
    nb, s, _ = rqkv.shape
    nt = s // TILE
    final = o_fwd is not None
    d = 1 if rev else 0
    if rev:
        cmap = lambda j: jnp.where(j < ctx_tiles, ctx_tiles - 1 - j, nt - 1 - (j - ctx_tiles))
    else:
        cmap = lambda j: j
    tok = lambda col: pl.BlockSpec((1, TILE, D_RET), lambda b, j: (b, cmap(j), col))
    in_specs = [tok(0), tok(1), tok(2),
                pl.BlockSpec((None, None, RET_HEADS, 1, 2 * RET_DK), lambda b, j: (layer, d, 0, 0, 0)),
                pl.BlockSpec((None, None, 1, D_RET), lambda b, j: (layer, d, 0, 0))]
    args = [rqkv, rqkv, rqkv, lg_head, lg_lane]
    if final:
        in_specs += [tok(0), tok(0), pl.BlockSpec((None, 1, D_RET), lambda b, j: (layer, 0, 0))]
        args += [o_fwd, rg, gn_g]
    return pl.pallas_call(
        functools.partial(_ret_kernel, rev=rev, final=final),
        grid=(nb, nt),
        in_specs=in_specs,
        out_specs=tok(0),
        out_shape=jax.ShapeDtypeStruct((nb, s, D_RET), BF16 if final else F32),
        scratch_shapes=[pltpu.VMEM((RET_HEADS, RET_CHUNK, RET_CHUNK), F32),
                        pltpu.VMEM((RET_CHUNK, D_RET), F32),
                        pltpu.VMEM((RET_CHUNK, D_RET), F32),
                        pltpu.VMEM((1, D_RET), F32),
                        pltpu.VMEM((RET_HEADS // 2, 2 * RET_DK, 2 * RET_DK), F32)],
        compiler_params=_params("parallel", "arbitrary"),
        name="retention_bwd" if rev else "retention_fwd",
    )(*args)


def _merge_kernel(x_ref, mod_ref, g1_ref, wg_ref, bg_ref, a_ref, att_ref, ret_ref,
                  wpa_ref, wpb_ref, wpc_ref, wo_ref, o_ref):
    mod = mod_ref[...]
    x = x_ref[0]
    h = _norm_mod(x, g1_ref[...], mod[0:1], mod[1:2]).astype(BF16)
    merged = None
    for n, (br_ref, w_ref) in enumerate(((a_ref, wpa_ref), (att_ref, wpb_ref), (ret_ref, wpc_ref))):
        sl = slice(n * D_MODEL, (n + 1) * D_MODEL)
        gate = _sigmoid(jnp.dot(h, wg_ref[:, sl], preferred_element_type=F32) + bg_ref[:, sl])
        term = gate * jnp.dot(br_ref[0], w_ref[...], preferred_element_type=F32)
        merged = term if merged is None else merged + term
    y = jnp.dot(merged.astype(BF16), wo_ref[...], preferred_element_type=F32)
    o_ref[0] = x + mod[2:3] * y


def _merge(layer, xa, mods, norm1_g, w_gates, b_gates, a, att, ret, w_pa, w_pb, w_pc, w_out,
           ctx_tiles, skip_ctx):
    nb, s, _ = xa.shape
    off = ctx_tiles if skip_ctx else 0
    nt = s // TILE - off
    wl = lambda shape: pl.BlockSpec((None,) + shape, lambda b, i: (layer,) + (0,) * len(shape))
    tok = lambda w: pl.BlockSpec((1, TILE, w), lambda b, i: (b, i + off, 0))
    return pl.pallas_call(
        _merge_kernel,
        grid=(nb, nt),
        in_specs=[tok(D_MODEL), _mod_spec(layer, nb, ctx_tiles, off), wl((1, D_MODEL)),
                  wl((D_MODEL, N_GATES)), wl((1, N_GATES)),
                  tok(D_CONV), tok(D_Q), tok(D_RET),
                  wl((D_CONV, D_MODEL)), wl((D_Q, D_MODEL)), wl((D_RET, D_MODEL)),
                  wl((D_MODEL, D_MODEL))],
        out_specs=tok(D_MODEL),
        out_shape=jax.ShapeDtypeStruct(xa.shape, F32),
        compiler_params=_params("parallel", "parallel"),
        name="merge",
    )(xa, mods, norm1_g, w_gates, b_gates, a, att, ret, w_pa, w_pb, w_pc, w_out)


def _ffn_kernel(x_ref, xp_ref, xn_ref, mod_ref, g2_ref, wu_ref, dw_ref, db_ref, wd_ref, gf_ref,
                o_ref, gs_ref, *, ctx_tiles, n_tiles, off, final):
    i = pl.program_id(1) + off
    first = (i == 0) | (i == ctx_tiles)
    last = (i == ctx_tiles - 1) | (i == n_tiles - 1)
    mod = mod_ref[...]
    x = x_ref[0]
    xe = jnp.concatenate([xp_ref[0], x, xn_ref[0]], axis=0)
    he = _norm_mod(xe, g2_ref[...], mod[3:4], mod[4:5]).astype(BF16)
    gt = jnp.dot(he, wu_ref[:, :D_FF], preferred_element_type=F32)
    gs_ref[0:FFN_HALO] = jnp.where(first, 0.0, gt[0:FFN_HALO])
    gs_ref[FFN_HALO:FFN_HALO + TILE] = gt[FFN_HALO:FFN_HALO + TILE]
    gs_ref[FFN_HALO + TILE:] = jnp.where(last, 0.0, gt[FFN_HALO + TILE:])
    val = jnp.dot(he[FFN_HALO:FFN_HALO + TILE], wu_ref[:, D_FF:], preferred_element_type=F32)
    conv = db_ref[...]
    for k in range(FFN_K):
        start = FFN_HALO + k - FFN_K // 2
        conv = conv + gs_ref[start:start + TILE, :] * dw_ref[k:k + 1, :]
    act = (_silu(conv) * val).astype(BF16)
    y = x + mod[5:6] * jnp.dot(act, wd_ref[...], preferred_element_type=F32)
    if final:
        y = y * lax.rsqrt(jnp.mean(y * y, axis=-1, keepdims=True) + EPS) * gf_ref[...]
    o_ref[0] = y


def _ffn(layer, xa, mods, norm2_g, w_up, dw_w, dw_b, w_down, final_g, ctx_tiles, final):
    nb, s, _ = xa.shape
    n_tiles = s // TILE
    off = ctx_tiles if final else 0
    nt = n_tiles - off
    hb = TILE // FFN_HALO
    wl = lambda shape: pl.BlockSpec((None,) + shape, lambda b, i: (layer,) + (0,) * len(shape))
    out_rows = s - off * TILE
    return pl.pallas_call(
        functools.partial(_ffn_kernel, ctx_tiles=ctx_tiles, n_tiles=n_tiles, off=off, final=final),
        grid=(nb, nt),
        in_specs=[pl.BlockSpec((1, TILE, D_MODEL), lambda b, i: (b, i + off, 0)),
                  pl.BlockSpec((1, FFN_HALO, D_MODEL),
                               lambda b, i: (b, jnp.maximum((i + off) * hb - 1, 0), 0)),
                  pl.BlockSpec((1, FFN_HALO, D_MODEL),
                               lambda b, i: (b, jnp.minimum((i + off + 1) * hb, n_tiles * hb - 1), 0)),
                  _mod_spec(layer, nb, ctx_tiles, off), wl((1, D_MODEL)),
                  wl((D_MODEL, 2 * D_FF)), wl((FFN_K, D_FF)), wl((1, D_FF)), wl((D_FF, D_MODEL)),
                  pl.BlockSpec((1, D_MODEL), lambda b, i: (0, 0))],
        out_specs=pl.BlockSpec((1, TILE, D_MODEL), lambda b, i: (b, i, 0)),
        out_shape=jax.ShapeDtypeStruct((nb, out_rows, D_MODEL), F32),
        scratch_shapes=[pltpu.VMEM((TILE + 2 * FFN_HALO, D_FF), F32)],
        compiler_params=_params("parallel", "parallel"),
        name="ffn_final" if final else "ffn",
    )(xa, xa, xa, mods, norm2_g, w_up, dw_w, dw_b, w_down, final_g)


def _rope_tables_t(ctx_len, seq_len):
    rows = seq_len // GRID_W
    row = jnp.repeat(jnp.arange(rows, dtype=F32), GRID_W)
    col = jnp.tile(jnp.arange(GRID_W, dtype=F32), rows)
    n_freq = ATT_HD // 4
    inv = ROPE_THETA ** (-jnp.arange(n_freq, dtype=F32) / n_freq)
    ang_r = inv[:, None] * row[None, :]
    ang_c = inv[:, None] * col[None, :]
    tabs = jnp.stack([jnp.cos(ang_r), jnp.sin(ang_r), jnp.cos(ang_c), jnp.sin(ang_c)])
    ident = jnp.stack([jnp.ones((n_freq, ctx_len), F32), jnp.zeros((n_freq, ctx_len), F32)] * 2)
    return jnp.concatenate([ident, tabs], axis=2)


def kernel(x, c, ctx, c_ctx, w_ada, b_ada, norm1_g, w_in, b_in, conv_dw_w, conv_dw_b, conv_ln_g, conv_ln_b, q_norm_g, k_norm_g, ret_decay_logit, ret_gn_g, w_pa, w_pb, w_pc, w_out, norm2_g, w_up, ffn_dw_w, ffn_dw_b, w_down, final_norm_g):
    nb, seq_len, _ = x.shape
    ctx_len = ctx.shape[1]
    assert ctx_len % TILE == 0 and seq_len % TILE == 0 and seq_len % GRID_W == 0
    ctx_tiles = ctx_len // TILE

    xa = jnp.concatenate([ctx, x], axis=1)
    cvec = jnp.zeros((MOD_ROWS, D_MODEL), F32).at[:nb].set(c).at[nb].set(c_ctx)
    rope_t = _rope_tables_t(ctx_len, seq_len)
    row3 = lambda a: a.reshape(DEPTH, 1, a.shape[-1])
    w_main = w_in[:, :, :C_GATES].astype(BF16)
    w_gates = w_in[:, :, C_GATES:].astype(BF16)
    b_main, b_gates = row3(b_in[:, :C_GATES]), row3(b_in[:, C_GATES:])
    qg = jnp.tile(q_norm_g, (1, ATT_HEADS))[:, :, None]
    kg = jnp.tile(k_norm_g, (1, ATT_KV_HEADS))[:, :, None]
    lg_head = jnp.broadcast_to(ret_decay_logit[:, :, :, None, None],
                               (DEPTH, 2, RET_HEADS, 1, 2 * RET_DK)).astype(F32)
    lg_lane = jnp.repeat(ret_decay_logit.astype(F32), RET_DK, axis=-1)[:, :, None, :]
    w_pa_b, w_pb_b, w_pc_b = w_pa.astype(BF16), w_pb.astype(BF16), w_pc.astype(BF16)
    w_out_b, w_up_b, w_down_b = w_out.astype(BF16), w_up.astype(BF16), w_down.astype(BF16)
    n1, n2 = row3(norm1_g), row3(norm2_g)
    cdb, clg, clb = row3(conv_dw_b), row3(conv_ln_g), row3(conv_ln_b)
    cdw = jnp.broadcast_to(conv_dw_w[:, :, None, :], (DEPTH, CONV_K, 8, D_CONV))
    fdb, gng = row3(ffn_dw_b), row3(ret_gn_g)
    final_g = final_norm_g.reshape(1, D_MODEL)

    mods = _ada_mod(cvec, w_ada, b_ada).reshape(DEPTH, MOD_ROWS, N_MOD, D_MODEL)

    for layer in range(DEPTH):
        last = layer == DEPTH - 1
        u, qt, k, vt, rqkv, rg = _in_proj(layer, xa, mods, n1, w_main, b_main, qg, kg, rope_t, ctx_tiles)
        a = _conv_module(layer, u, cdw, cdb, clg, clb, ctx_tiles)
        att = _attention(qt, k, vt, ctx_tiles, ("base", "u8", "bf16exp", "vpusum")[layer])
        o_fwd = _retention(layer, rqkv, lg_head, lg_lane, rev=False, ctx_tiles=ctx_tiles)
        ret = _retention(layer, rqkv, lg_head, lg_lane, o_fwd, rg, gng, rev=True, ctx_tiles=ctx_tiles)
        xm = _merge(layer, xa, mods, n1, w_gates, b_gates, a, att, ret,
                    w_pa_b, w_pb_b, w_pc_b, w_out_b, ctx_tiles, skip_ctx=False)
        xa = _ffn(layer, xm, mods, n2, w_up_b, ffn_dw_w, fdb, w_down_b, final_g, ctx_tiles, final=last)
    return xa
```
